```python
import math
import jax, jax.numpy as jnp
from jax import lax
import numpy as np


D_MODEL = 1024
BATCH = 2
SEQ = 8192
DEPTH = 4
DEC_BATCH = 128
DEC_SEQ = 1
PAST_LEN = 2048
PAGE_SIZE = 128

N_A = DEPTH // 2
N_B = DEPTH - N_A
HGRN_EXPAND = 128
HGRN_HEADS = D_MODEL // HGRN_EXPAND
HGRN_HEAD_I = D_MODEL // HGRN_HEADS
HGRN_F_DIM = HGRN_HEADS * HGRN_EXPAND
HGRN_CHUNK = 64
LB_FLOOR = 1e-30
DIFF_HEADS = 8
DIFF_HEAD_DIM = D_MODEL // DIFF_HEADS // 2
DIFF_V_DIM = 2 * DIFF_HEAD_DIM
ROPE_THETA = 10000.0
Q_BLOCK = 128
NEG_BIG = -1e30
N_EXPERTS = 32
TOP_K = 4
D_EXPERT = D_MODEL
SWIGLU_LIMIT = 7.0
SWIGLU_ALPHA = 1.702
MOE_BLOCK = 128
DEEPNORM_ALPHA = (2 * DEPTH) ** 0.25
DEEPNORM_BETA = (8 * DEPTH) ** -0.25
LN_EPS = 1e-5
RMS_EPS = 1e-5

kernel_name = "yoco_hgrn2_diffattn_moe_step"


def layer_norm(x, g, b):
    xf = x.astype(jnp.float32)
    mu = xf.mean(-1, keepdims=True)
    var = jnp.square(xf - mu).mean(-1, keepdims=True)
    return ((xf - mu) * lax.rsqrt(var + LN_EPS) * g + b).astype(x.dtype)


def rms_norm(x, w):
    xf = x.astype(jnp.float32)
    return xf * lax.rsqrt(jnp.mean(jnp.square(xf), -1, keepdims=True) + RMS_EPS) * w


def rotary(x, pos):
    half = x.shape[-1] // 2
    inv_freq = ROPE_THETA ** (-jnp.arange(half, dtype=jnp.float32) / half)
    ang = pos.astype(jnp.float32)[:, None] * inv_freq[None, :]
    bshape = (pos.shape[0],) + (1,) * (x.ndim - 3) + (half,)
    cos = jnp.cos(ang).reshape(bshape)
    sin = jnp.sin(ang).reshape(bshape)
    xf = x.astype(jnp.float32)
    x1, x2 = xf[..., :half], xf[..., half:]
    return jnp.concatenate([x1 * cos - x2 * sin, x2 * cos + x1 * sin], -1).astype(x.dtype)


def adaln(c, w, b, n):
    m = jax.nn.silu(c) @ w + b
    return jnp.split(m[:, None, :], n, axis=-1)


def hgrn2_mixer(h, s0, w_in, lb, norm_w, w_out, chunk):
    B, T, _ = h.shape
    q, f, i, g = jnp.split(h @ w_in, [HGRN_F_DIM, 2 * HGRN_F_DIM, 2 * HGRN_F_DIM + D_MODEL], axis=-1)
    f = f.astype(jnp.float32)
    log_f = jnp.logaddexp(jnp.log(jnp.maximum(lb, LB_FLOOR)), jnp.log1p(-lb) + jax.nn.log_sigmoid(f))
    k = (1.0 - lb) * jax.nn.sigmoid(-f)
    q = jax.nn.silu(q.astype(jnp.float32))
    v = i.astype(jnp.float32)
    nc = T // chunk

    def to_chunks(a, d):
        return a.reshape(B, nc, chunk, HGRN_HEADS, d).transpose(1, 0, 3, 2, 4)

    xs = (to_chunks(q, HGRN_EXPAND), to_chunks(k, HGRN_EXPAND),
          to_chunks(log_f, HGRN_EXPAND), to_chunks(v, HGRN_HEAD_I))
    causal = jnp.tril(jnp.ones((chunk, chunk), dtype=bool))[:, :, None]

    def step(S, xc):
        qc, kc, lfc, vc = xc
        b = jnp.cumsum(lfc, axis=-2)
        o_inter = jnp.einsum('bhtk,bhkv->bhtv', qc * jnp.exp(b), S)
        d = b[:, :, :, None, :] - b[:, :, None, :, :]
        decay = jnp.where(causal, jnp.exp(jnp.minimum(d, 0.0)), 0.0)
        scores = jnp.einsum('bhtk,bhsk,bhtsk->bhts', qc, kc, decay)
        o = o_inter + jnp.einsum('bhts,bhsv->bhtv', scores, vc)
        b_last = b[:, :, -1:, :]
        S_new = jnp.exp(b_last[:, :, 0, :, None]) * S + jnp.einsum('bhsk,bhsv->bhkv', kc * jnp.exp(b_last - b), vc)
        return S_new, o

    s_fin, o = lax.scan(step, s0.astype(jnp.float32), xs)
    o = o.transpose(1, 0, 3, 2, 4).reshape(B, T, HGRN_HEADS, HGRN_HEAD_I)
    gate = jax.nn.silu(g.astype(jnp.float32)).reshape(B, T, HGRN_HEADS, HGRN_HEAD_I)
    o = rms_norm(o, norm_w) * gate
    y = o.reshape(B, T, D_MODEL).astype(h.dtype) @ w_out
    return y, s_fin.astype(s0.dtype)


def shared_kv(x, c, pos, kv_ada_w, kv_ada_b, kv_w):
    shift, scale = adaln(c, kv_ada_w, kv_ada_b, 2)
    h = x * (1 + scale) + shift
    k, v = jnp.split(h @ kv_w, 2, axis=-1)
    B, T, _ = x.shape
    k = rotary(k.reshape(B, T, DIFF_HEADS, 2, DIFF_HEAD_DIM), pos).reshape(B, T, DIFF_HEADS, DIFF_V_DIM)
    v = v.reshape(B, T, DIFF_HEADS, DIFF_V_DIM)
    return k, v


def diff_attention(h, keys, vals, q_pos, k_pos, w_q, lam_vec, subln_w, w_out, lam_init):
    B, T, _ = h.shape
    q = rotary((h @ w_q).reshape(B, T, DIFF_HEADS, 2, DIFF_HEAD_DIM), q_pos)
    kk = keys.reshape(B, keys.shape[1], DIFF_HEADS, 2, DIFF_HEAD_DIM)
    lv = lam_vec.astype(jnp.float32)
    lam = jnp.exp(jnp.sum(lv[0] * lv[1])) - jnp.exp(jnp.sum(lv[2] * lv[3])) + lam_init
    qb = min(Q_BLOCK, T)
    nb = T // qb
    q_blocks = q.reshape(B, nb, qb, DIFF_HEADS, 2, DIFF_HEAD_DIM).transpose(1, 0, 2, 3, 4, 5)
    pos_blocks = q_pos.reshape(nb, qb)
    scale = DIFF_HEAD_DIM ** -0.5

    def block(args):
        qblk, pblk = args
        s = jnp.einsum('bqhcd,bkhcd->bchqk', qblk, kk).astype(jnp.float32) * scale
        s = jnp.where(k_pos[None, :] <= pblk[:, None], s, NEG_BIG)
        p = jax.nn.softmax(s, axis=-1)
        a = p[:, 0] - lam * p[:, 1]
        return jnp.einsum('bhqk,bkhe->bqhe', a.astype(vals.dtype), vals)

    o = lax.map(block, (q_blocks, pos_blocks))
    o = o.transpose(1, 0, 2, 3, 4).reshape(B, T, DIFF_HEADS, DIFF_V_DIM)
    o = rms_norm(o, subln_w) * (1.0 - lam_init)
    return o.reshape(B, T, D_MODEL).astype(h.dtype) @ w_out


def expert_ffn(xb, wgu, bgu, wd, bd):
    gu = xb @ wgu + bgu
    gate, up = gu[..., ::2], gu[..., 1::2]
    gate = jnp.minimum(gate, SWIGLU_LIMIT)
    up = jnp.clip(up, -SWIGLU_LIMIT, SWIGLU_LIMIT)
    glu = gate * jax.nn.sigmoid(gate * SWIGLU_ALPHA)
    return ((up + 1) * glu) @ wd + bd


def moe_ffn(h, w_r, b_r, w_gu, b_gu, w_down, b_down):
    B, T, D = h.shape
    xt = h.reshape(B * T, D)
    n = xt.shape[0]
    logits = (xt @ w_r + b_r).astype(jnp.float32)
    top_logit, top_idx = lax.top_k(logits, TOP_K)
    gates = jax.nn.softmax(top_logit, axis=-1)
    n_slots = n * TOP_K
    blk = min(MOE_BLOCK, max(8, n_slots // N_EXPERTS))
    n_blocks = -(-(n_slots + N_EXPERTS * (blk - 1)) // blk)
    flat_e = top_idx.reshape(-1).astype(jnp.int32)
    order = jnp.argsort(flat_e)
    sorted_e = flat_e[order]
    counts = jnp.bincount(flat_e, length=N_EXPERTS).astype(jnp.int32)
    padded = (counts + blk - 1) // blk * blk
    pad_end = jnp.cumsum(padded)
    pad_start = pad_end - padded
    start = jnp.cumsum(counts) - counts
    dest_sorted = pad_start[sorted_e] + jnp.arange(n_slots, dtype=jnp.int32) - start[sorted_e]
    dest = jnp.zeros((n_slots,), jnp.int32).at[order].set(dest_sorted.astype(jnp.int32))
    slot_token = jnp.full((n_blocks * blk,), n, jnp.int32).at[dest].set(
        jnp.arange(n_slots, dtype=jnp.int32) // TOP_K)
    block_expert = jnp.minimum(
        jnp.searchsorted(pad_end, jnp.arange(n_blocks, dtype=jnp.int32) * blk, side='right'), N_EXPERTS - 1)
    x_pad = jnp.concatenate([xt, jnp.zeros((1, D), xt.dtype)], axis=0)
    xb = x_pad[slot_token].reshape(n_blocks, blk, D)

    def run_block(args):
        xblk, e = args
        return expert_ffn(xblk, w_gu[e], b_gu[e], w_down[e], b_down[e])

    yb = lax.map(run_block, (xb, block_expert)).reshape(n_blocks * blk, D)
    y = yb[dest].reshape(n, TOP_K, D)
    out = jnp.einsum('nk,nkd->nd', gates.astype(y.dtype), y)
    return out.reshape(B, T, D)


def trunk(x, c, past_len, s0, past_k, past_v, chunk, p):
    T = x.shape[1]
    q_pos = past_len + jnp.arange(T, dtype=jnp.int32)
    k_pos = jnp.arange(past_len + T, dtype=jnp.int32)
    states = []
    k_new = v_new = keys = vals = None
    for l in range(DEPTH):
        sh_m, sc_m, g_m, sh_f, sc_f, g_f = adaln(c, p['ada_w'][l], p['ada_b'][l], 6)
        h = x * (1 + sc_m) + sh_m
        if l < N_A:
            y, s_l = hgrn2_mixer(h, s0[l], p['hgrn_w_in'][l], p['lb'][l], p['hgrn_norm_w'][l],
                                 p['hgrn_w_out'][l], chunk)
            states.append(s_l)
        else:
            if l == N_A:
                k_new, v_new = shared_kv(x, c, q_pos, p['kv_ada_w'], p['kv_ada_b'], p['kv_w'])
                if past_k is None:
                    keys, vals = k_new, v_new
                else:
                    keys = jnp.concatenate([past_k, k_new.astype(past_k.dtype)], axis=1)
                    vals = jnp.concatenate([past_v, v_new.astype(past_v.dtype)], axis=1)
            j = l - N_A
            lam_init = 0.8 - 0.6 * math.exp(-0.3 * l)
            y = diff_attention(h, keys, vals, q_pos, k_pos, p['diff_w_q'][j], p['diff_lambda'][j],
                               p['diff_subln_w'][j], p['diff_w_out'][j], lam_init)
        x = layer_norm(DEEPNORM_ALPHA * x + g_m * y, p['ln_g'][l, 0], p['ln_b'][l, 0])
        h = x * (1 + sc_f) + sh_f
        y = moe_ffn(h, p['moe_w_router'][l], p['moe_b_router'][l], p['moe_w_gu'][l], p['moe_b_gu'][l],
                    p['moe_w_down'][l], p['moe_b_down'][l])
        x = layer_norm(DEEPNORM_ALPHA * x + g_f * y, p['ln_g'][l, 1], p['ln_b'][l, 1])
    return x, k_new, v_new, jnp.stack(states)


def setup_inputs(seed: int = 0) -> dict:
    key = jax.random.key(seed)
    ks = iter(jax.random.split(key, 48))

    def nrm(shape, scale):
        return jax.random.normal(next(ks), shape, jnp.float32) * scale

    n_pages = PAST_LEN // PAGE_SIZE
    n_phys = (5 * DEC_BATCH * n_pages) // 4
    inv = D_MODEL ** -0.5
    beta = DEEPNORM_BETA
    page_table = jax.random.permutation(next(ks), n_phys)[:DEC_BATCH * n_pages].reshape(
        DEC_BATCH, n_pages).astype(jnp.int32)
    hgrn_col_scale = jnp.concatenate([jnp.full((2 * HGRN_F_DIM,), inv), jnp.full((D_MODEL,), beta * inv),
                                      jnp.full((D_MODEL,), inv)])
    kv_col_scale = jnp.concatenate([jnp.full((D_MODEL,), inv), jnp.full((D_MODEL,), beta * inv)])
    return {
        'x_prompt': nrm((BATCH, SEQ, D_MODEL), 1.0),
        'x_sample': nrm((DEC_BATCH, DEC_SEQ, D_MODEL), 1.0),
        'cache_k': nrm((n_phys, PAGE_SIZE, DIFF_HEADS, DIFF_V_DIM), 1.0),
        'cache_v': nrm((n_phys, PAGE_SIZE, DIFF_HEADS, DIFF_V_DIM), 0.5),
        'state_hgrn': nrm((N_A, DEC_BATCH, HGRN_HEADS, HGRN_EXPAND, HGRN_HEAD_I), 0.3),
        'page_table': page_table,
        'c_prompt': nrm((BATCH, D_MODEL), 1.0),
        'c_sample': nrm((DEC_BATCH, D_MODEL), 1.0),
        'ada_w': nrm((DEPTH, D_MODEL, 6 * D_MODEL), 0.5 * inv),
        'ada_b': nrm((DEPTH, 6 * D_MODEL), 0.02),
        'ln_g': 1.0 + nrm((DEPTH, 2, D_MODEL), 0.02),
        'ln_b': nrm((DEPTH, 2, D_MODEL), 0.02),
        'hgrn_w_in': nrm((N_A, D_MODEL, 2 * HGRN_F_DIM + 2 * D_MODEL), 1.0) * hgrn_col_scale,
        'hgrn_lb_logits': nrm((N_A, HGRN_F_DIM), 1.0),
        'hgrn_norm_w': 1.0 + nrm((N_A, HGRN_HEAD_I), 0.02),
        'hgrn_w_out': nrm((N_A, D_MODEL, D_MODEL), beta * inv),
        'kv_ada_w': nrm((D_MODEL, 2 * D_MODEL), 0.5 * inv),
        'kv_ada_b': nrm((2 * D_MODEL,), 0.02),
        'kv_w': nrm((D_MODEL, 2 * D_MODEL), 1.0) * kv_col_scale,
        'diff_w_q': nrm((N_B, D_MODEL, DIFF_HEADS * 2 * DIFF_HEAD_DIM), inv),
        'diff_lambda': nrm((N_B, 4, DIFF_HEAD_DIM), 0.1),
        'diff_subln_w': 1.0 + nrm((N_B, DIFF_V_DIM), 0.02),
        'diff_w_out': nrm((N_B, D_MODEL, D_MODEL), beta * inv),
        'moe_w_router': nrm((DEPTH, D_MODEL, N_EXPERTS), inv),
        'moe_b_router': nrm((DEPTH, N_EXPERTS), 0.01),
        'moe_w_gu': nrm((DEPTH, N_EXPERTS, D_MODEL, 2 * D_EXPERT), beta * inv),
        'moe_b_gu': nrm((DEPTH, N_EXPERTS, 2 * D_EXPERT), 0.02),
        'moe_w_down': nrm((DEPTH, N_EXPERTS, D_EXPERT, D_MODEL), beta * D_EXPERT ** -0.5),
        'moe_b_down': nrm((DEPTH, N_EXPERTS, D_MODEL), 0.02),
    }


def reference(x_prompt, x_sample, cache_k, cache_v, state_hgrn, page_table, c_prompt, c_sample,
              ada_w, ada_b, ln_g, ln_b, hgrn_w_in, hgrn_lb_logits, hgrn_norm_w, hgrn_w_out,
              kv_ada_w, kv_ada_b, kv_w, diff_w_q, diff_lambda, diff_subln_w, diff_w_out,
              moe_w_router, moe_b_router, moe_w_gu, moe_b_gu, moe_w_down, moe_b_down):
    sm = jax.nn.softmax(hgrn_lb_logits.astype(jnp.float32), axis=0)
    lb = jnp.concatenate([jnp.zeros_like(sm[0:1]), jnp.cumsum(sm, axis=0)[:-1]], axis=0)
    lb = jnp.clip(lb, 0.0, 1.0 - 1e-6)
    p = dict(ada_w=ada_w, ada_b=ada_b, ln_g=ln_g, ln_b=ln_b, hgrn_w_in=hgrn_w_in, lb=lb,
             hgrn_norm_w=hgrn_norm_w, hgrn_w_out=hgrn_w_out, kv_ada_w=kv_ada_w, kv_ada_b=kv_ada_b,
             kv_w=kv_w, diff_w_q=diff_w_q, diff_lambda=diff_lambda, diff_subln_w=diff_subln_w,
             diff_w_out=diff_w_out, moe_w_router=moe_w_router, moe_b_router=moe_b_router,
             moe_w_gu=moe_w_gu, moe_b_gu=moe_b_gu, moe_w_down=moe_w_down, moe_b_down=moe_b_down)
    b_p, t_p, _ = x_prompt.shape
    s0_prompt = jnp.zeros((N_A, b_p, HGRN_HEADS, HGRN_EXPAND, HGRN_HEAD_I), x_prompt.dtype)
    y_prompt, k_prompt, v_prompt, state_prompt = trunk(
        x_prompt, c_prompt, 0, s0_prompt, None, None, min(HGRN_CHUNK, t_p), p)
    b_d, n_pages = page_table.shape
    past_len = n_pages * PAGE_SIZE
    past_k = cache_k[page_table].reshape(b_d, past_len, DIFF_HEADS, DIFF_V_DIM)
    past_v = cache_v[page_table].reshape(b_d, past_len, DIFF_HEADS, DIFF_V_DIM)
    y_sample, k_sample, v_sample, state_sample = trunk(
        x_sample, c_sample, past_len, state_hgrn, past_k, past_v, x_sample.shape[1], p)
    return (y_prompt, y_sample, k_prompt, v_prompt, state_prompt, k_sample, v_sample, state_sample)
```

```python
import functools
import math

import jax
import jax.numpy as jnp
from jax import lax
from jax.experimental import pallas as pl
from jax.experimental.pallas import tpu as pltpu

F32 = jnp.float32
BF16 = jnp.bfloat16
I32 = jnp.int32

LANES = 128
HGRN_CHUNK = 64
HGRN_SUB = 16
LB_FLOOR = 1e-30
DIFF_HEAD_DIM = 64
ROPE_THETA = 10000.0
ROPE_HALF = DIFF_HEAD_DIM // 2
NEG_BIG = -1e30
N_EXPERTS = 32
TOP_K = 4
SWIGLU_LIMIT = 7.0
SWIGLU_ALPHA = 1.702
LN_EPS = 1e-5
RMS_EPS = 1e-5
VMEM_LIMIT = 56 * 1024 * 1024


def _params(*sem):
    return pltpu.CompilerParams(dimension_semantics=sem, vmem_limit_bytes=VMEM_LIMIT)


def _sigmoid(x):
    return 1.0 / (1.0 + jnp.exp(-x))


def _silu(x):
    return x * _sigmoid(x)


def _log_sigmoid(x):
    return jnp.minimum(x, 0.0) - jnp.log1p(jnp.exp(-jnp.abs(x)))


def _layer_norm(z, g, b):
    mu = jnp.mean(z, axis=-1, keepdims=True)
    zc = z - mu
    var = jnp.mean(zc * zc, axis=-1, keepdims=True)
    return zc * lax.rsqrt(var + LN_EPS) * g + b


def _dot(a, b):
    return jnp.dot(a, b, preferred_element_type=F32)


def _dot_nt(a, b):
    return lax.dot_general(a, b, (((1,), (1,)), ((), ())), preferred_element_type=F32)


def _dot_tn(a, b):
    return lax.dot_general(a, b, (((0,), (0,)), ((), ())), preferred_element_type=F32)


def _split3(x):
    hi = x.astype(BF16)
    r1 = x - hi.astype(F32)
    mid = r1.astype(BF16)
    lo = (r1 - mid.astype(F32)).astype(BF16)
    return hi, mid, lo


def _adaln_kernel(c_ref, w_ref, b_ref, o_ref):
    s = _silu(c_ref[...]).astype(BF16)
    o_ref[0] = _dot(s, w_ref[0].astype(BF16)) + b_ref[0]


def adaln(c, w, b, tn=1024):
    m, d = c.shape
    nl, _, no = w.shape
    return pl.pallas_call(
        _adaln_kernel,
        out_shape=jax.ShapeDtypeStruct((nl, m, no), F32),
        grid=(nl, no // tn),
        in_specs=[
            pl.BlockSpec((m, d), lambda l, j: (0, 0)),
            pl.BlockSpec((1, d, tn), lambda l, j: (l, 0, j)),
            pl.BlockSpec((1, 1, tn), lambda l, j: (l, 0, j)),
        ],
        out_specs=pl.BlockSpec((1, m, tn), lambda l, j: (l, 0, j)),
        compiler_params=_params("parallel", "parallel"),
        name="adaln",
    )(c, w, b.reshape(nl, 1, no))


def _lb_kernel(logit_ref, loglb_ref, log1m_ref, onem_ref):
    x = logit_ref[...]
    nl = x.shape[0]
    e = jnp.exp(x - jnp.max(x, axis=0, keepdims=True))
    sm = e / jnp.sum(e, axis=0, keepdims=True)
    acc = jnp.zeros_like(sm[0:1])
    for l in range(nl):
        lb = jnp.clip(acc, 0.0, 1.0 - 1e-6)
        loglb_ref[l] = jnp.log(jnp.maximum(lb, LB_FLOOR))
        log1m_ref[l] = jnp.log1p(-lb)
        onem_ref[l] = 1.0 - lb
        acc = acc + sm[l:l + 1]


def hgrn_lower_bounds(lb_logits):
    nl, f = lb_logits.shape
    shp = jax.ShapeDtypeStruct((nl, 1, f), F32)
    return pl.pallas_call(_lb_kernel, out_shape=(shp, shp, shp), name="hgrn_lb")(lb_logits)


def _rotary_tile(y, cos, sin_signed):
    lane = lax.broadcasted_iota(I32, (1, LANES), 1)
    first_half = (lane % DIFF_HEAD_DIM) < ROPE_HALF
    partner = jnp.where(first_half, pltpu.roll(y, LANES - ROPE_HALF, 1), pltpu.roll(y, ROPE_HALF, 1))
    return y * cos + partner * sin_signed


def _modmm_kernel(*refs, rotary, out_scale, n_out):
    if rotary:
        x_ref, sc_ref, sh_ref, w_ref, cos_ref, sin_ref = refs[:6]
        rest = refs[6:]
    else:
        x_ref, sc_ref, sh_ref, w_ref = refs[:4]
        rest = refs[4:]
    out_refs, h_ref = rest[:n_out], rest[n_out]

    @pl.when(pl.program_id(1) == 0)
    def _():
        h_ref[...] = (x_ref[...] * (1.0 + sc_ref[0]) + sh_ref[0]).astype(BF16)

    y = _dot(h_ref[...], w_ref[...])
    if rotary:
        cos, sin = cos_ref[...], sin_ref[...]
        y = jnp.concatenate(
            [_rotary_tile(y[:, g * LANES:(g + 1) * LANES], cos, sin) for g in range(y.shape[1] // LANES)], axis=1)
    if out_scale != 1.0:
        y = y * out_scale
    for o_ref in out_refs:
        o_ref[...] = y.astype(o_ref.dtype)


def modmm(x, sc, sh, w, *, rows_per_group, tm, out_dtypes, rope=None, out_scale=1.0, tn=1024):
    n, d = x.shape
    no = w.shape[1]
    tiles_per_group = rows_per_group // tm
    r = sc.shape[1]
    in_specs = [
        pl.BlockSpec((tm, d), lambda i, j: (i, 0)),
        pl.BlockSpec((1, r, d), lambda i, j: (i // tiles_per_group, 0, 0)),
        pl.BlockSpec((1, r, d), lambda i, j: (i // tiles_per_group, 0, 0)),
        pl.BlockSpec((d, tn), lambda i, j: (0, j)),
    ]
    args = [x, sc, sh, w]
    if rope is not None:
        pos_tiles = rope[0].shape[0] // tm
        in_specs += [pl.BlockSpec((tm, LANES), lambda i, j: (i % pos_tiles, 0))] * 2
        args += list(rope)
    outs = tuple(jax.ShapeDtypeStruct((n, no), dt) for dt in out_dtypes)
    res = pl.pallas_call(
        functools.partial(_modmm_kernel, rotary=rope is not None, out_scale=out_scale, n_out=len(outs)),
        out_shape=outs,
        grid=(n // tm, no // tn),
        in_specs=in_specs,
        out_specs=tuple(pl.BlockSpec((tm, tn), lambda i, j: (i, j)) for _ in outs),
        scratch_shapes=[pltpu.VMEM((tm, d), BF16)],
        compiler_params=_params("parallel", "arbitrary"),
        name="modmm",
    )(*args)
    return res


def _hgrn_chunk_kernel(q_ref, f_ref, v_ref, g_ref, loglb_ref, log1m_ref, onem_ref, nw_ref, s0_ref,
                       o_ref, sfin_ref, st_ref, *, n_chunks):
    t = pl.program_id(2)
    c, sub = HGRN_CHUNK, HGRN_SUB

    @pl.when(t == 0)
    def _():
        st_ref[...] = s0_ref[0, 0].T

    loglb, log1m, onem, nw = loglb_ref[0], log1m_ref[0], onem_ref[0], nw_ref[...]
    row = lax.broadcasted_iota(I32, (c, c), 0)
    col = lax.broadcasted_iota(I32, (c, c), 1)
    tri = jnp.where(col <= row, 1.0, 0.0).astype(BF16)
    sub_row = lax.broadcasted_iota(I32, (sub, 1), 0)

    def chunk(ci, carry):
        rows = pl.ds(pl.multiple_of(ci * c, c), c)
        f = f_ref[rows, :]
        q = _silu(q_ref[rows, :])
        v = v_ref[rows, :]
        x1 = loglb
        x2 = log1m + _log_sigmoid(f)
        mx = jnp.maximum(x1, x2)
        logf = mx + jnp.log1p(jnp.exp(jnp.minimum(x1, x2) - mx))
        kk = onem * _sigmoid(-f)
        hi, mid, lo = _split3(logf)
        b = _dot(tri, hi) + _dot(tri, mid) + _dot(tri, lo)
        st = st_ref[...]
        v16 = v.astype(BF16)
        o_inter = _dot_nt((q * jnp.exp(b)).astype(BF16), st.astype(BF16))
        o_parts = []
        for i in range(c // sub):
            sl = slice(i * sub, (i + 1) * sub)
            qi, bi, ki, vi = q[sl], b[sl], kk[sl], v[sl]
            oi = o_inter[sl]
            for s in range(sub):
                w = qi * ki[s:s + 1] * jnp.exp(jnp.minimum(bi - bi[s:s + 1], 0.0))
                a = jnp.sum(w, axis=1, keepdims=True)
                a = jnp.where(sub_row >= s, a, 0.0)
                oi = oi + a * vi[s:s + 1]
            if i > 0:
                r = b[i * sub - 1:i * sub]
                qs = (qi * jnp.exp(bi - r)).astype(BF16)
                ks = (kk[:i * sub] * jnp.exp(r - b[:i * sub])).astype(BF16)
                a = _dot_nt(qs, ks)
                oi = oi + _dot(a.astype(BF16), v16[:i * sub])
            o_parts.append(oi)
        o = jnp.concatenate(o_parts, axis=0)
        b_last = b[c - 1:c]
        kd = (kk * jnp.exp(b_last - b)).astype(BF16)
        st_ref[...] = st * jnp.exp(b_last) + _dot_tn(v16, kd)
        gate = _silu(g_ref[rows, :])
        o = o * lax.rsqrt(jnp.mean(o * o, axis=-1, keepdims=True) + RMS_EPS) * nw * gate
        o_ref[rows, :] = o.astype(o_ref.dtype)
        return carry

    lax.fori_loop(0, n_chunks, chunk, 0)

    @pl.when(t == pl.num_programs(2) - 1)
    def _():
        sfin_ref[0, 0] = st_ref[...].T


def hgrn_chunked(proj, loglb, log1m, onem, norm_w, s0, *, batch, seq, tb):
    n = proj.shape[0]
    heads = proj.shape[1] // (4 * LANES)
    tpb = seq // tb
    row_blk = lambda off: pl.BlockSpec((tb, LANES), lambda b, h, t: (b * tpb + t, off * heads + h))
    vec = pl.BlockSpec((1, 1, LANES), lambda b, h, t: (0, 0, h))
    st_spec = pl.BlockSpec((1, 1, LANES, LANES), lambda b, h, t: (b, h, 0, 0))
    return pl.pallas_call(
        functools.partial(_hgrn_chunk_kernel, n_chunks=tb // HGRN_CHUNK),
        out_shape=(jax.ShapeDtypeStruct((n, heads * LANES), BF16),
                   jax.ShapeDtypeStruct(s0.shape, F32)),
        grid=(batch, heads, tpb),
        in_specs=[row_blk(0), row_blk(1), row_blk(2), row_blk(3), vec, vec, vec,
                  pl.BlockSpec((1, LANES), lambda b, h, t: (0, 0)), st_spec],
        out_specs=(pl.BlockSpec((tb, LANES), lambda b, h, t: (b * tpb + t, h)), st_spec),
        scratch_shapes=[pltpu.VMEM((LANES, LANES), F32)],
        compiler_params=_params("parallel", "parallel", "arbitrary"),
        name="hgrn_chunked",
    )(proj, proj, proj, proj, loglb, log1m, onem, norm_w, s0)


def _hgrn_step_kernel(q_ref, f_ref, v_ref, g_ref, loglb_ref, log1m_ref, onem_ref, nw_ref, s_ref,
                      o_ref, snew_ref, *, bb):
    f = f_ref[...]
    q = _silu(q_ref[...])
    v = v_ref[...]
    x1 = loglb_ref[0]
    x2 = log1m_ref[0] + _log_sigmoid(f)
    mx = jnp.maximum(x1, x2)
    decay = jnp.exp(mx + jnp.log1p(jnp.exp(jnp.minimum(x1, x2) - mx)))
    kk = onem_ref[0] * _sigmoid(-f)
    q_t, d_t, k_t = q.T, decay.T, kk.T
    rows = []
    for e in range(bb):
        s_new = s_ref[e, 0] * d_t[:, e:e + 1] + k_t[:, e:e + 1] * v[e:e + 1, :]
        snew_ref[e, 0] = s_new
        rows.append(jnp.sum(q_t[:, e:e + 1] * s_new, axis=0, keepdims=True))
    o = jnp.concatenate(rows, axis=0)
    o = o * lax.rsqrt(jnp.mean(o * o, axis=-1, keepdims=True) + RMS_EPS) * nw_ref[...] * _silu(g_ref[...])
    o_ref[...] = o.astype(o_ref.dtype)


def hgrn_step(proj, loglb, log1m, onem, norm_w, state, *, bb=16):
    n = proj.shape[0]
    heads = proj.shape[1] // (4 * LANES)
    row_blk = lambda off: pl.BlockSpec((bb, LANES), lambda i, h: (i, off * heads + h))
    vec = pl.BlockSpec((1, 1, LANES), lambda i, h: (0, 0, h))
    st_spec = pl.BlockSpec((bb, 1, LANES, LANES), lambda i, h: (i, h, 0, 0))
    return pl.pallas_call(
        functools.partial(_hgrn_step_kernel, bb=bb),
        out_shape=(jax.ShapeDtypeStruct((n, heads * LANES), BF16),
                   jax.ShapeDtypeStruct(state.shape, F32)),
        grid=(n // bb, heads),
        in_specs=[row_blk(0), row_blk(1), row_blk(2), row_blk(3), vec, vec, vec,
                  pl.BlockSpec((1, LANES), lambda i, h: (0, 0)), st_spec],
        out_specs=(pl.BlockSpec((bb, LANES), lambda i, h: (i, h)), st_spec),
        compiler_params=_params("parallel", "parallel"),
        name="hgrn_step",
    )(proj, proj, proj, proj, loglb, log1m, onem, norm_w, state)


def _proj_ln_kernel(a_ref, w_ref, x_ref, gm_ref, lg_ref, lb_ref, o_ref, *, alpha):
    y = _dot(a_ref[...], w_ref[...])
    o_ref[...] = _layer_norm(alpha * x_ref[...] + gm_ref[0] * y, lg_ref[...], lb_ref[...])


def proj_ln(a, w, x, gm, ln_g, ln_b, *, rows_per_group, tm, alpha):
    n, d = x.shape
    tiles_per_group = rows_per_group // tm
    r = gm.shape[1]
    row = pl.BlockSpec((tm, d), lambda i: (i, 0))
    vec = pl.BlockSpec((1, d), lambda i: (0, 0))
    return pl.pallas_call(
        functools.partial(_proj_ln_kernel, alpha=alpha),
        out_shape=jax.ShapeDtypeStruct((n, d), F32),
        grid=(n // tm,),
        in_specs=[row, pl.BlockSpec((d, d), lambda i: (0, 0)), row,
                  pl.BlockSpec((1, r, d), lambda i: (i // tiles_per_group, 0, 0)), vec, vec],
        out_specs=row,
        compiler_params=_params("parallel"),
        name="proj_ln",
    )(a, w, x, gm, ln_g.reshape(1, d), ln_b.reshape(1, d))


def _lambda_value(lam_ref, lam_init):
    lv = lam_ref[...]
    s1 = jnp.sum(lv[0:1] * lv[1:2], axis=1, keepdims=True)
    s2 = jnp.sum(lv[2:3] * lv[3:4], axis=1, keepdims=True)
    return jnp.exp(s1) - jnp.exp(s2) + lam_init


def _attn_prefill_kernel(lam_ref, q_ref, k_ref, v_ref, w_ref, o_ref, *, tq, lam_init):
    i = pl.program_id(2)
    q = q_ref[...]
    lane = lax.broadcasted_iota(I32, (1, LANES), 1)
    zero = jnp.zeros_like(q)
    q1 = jnp.where(lane < DIFF_HEAD_DIM, q, zero)
    q2 = jnp.where(lane < DIFF_HEAD_DIM, zero, q)
    keep = lax.broadcasted_iota(I32, (tq, tq), 1) <= lax.broadcasted_iota(I32, (tq, tq), 0)

    def update(s, vj, m, l, acc):
        m_new = jnp.maximum(m, jnp.max(s, axis=1, keepdims=True))
        alpha = jnp.exp(m - m_new)
        p = jnp.exp(s - m_new)
        l = alpha * l + jnp.sum(p, axis=1, keepdims=True)
        acc = alpha * acc + _dot(p.astype(BF16), vj)
        return m_new, l, acc

    def block(j, carry, masked):
        rows = pl.ds(pl.multiple_of(j * tq, tq), tq)
        kj, vj = k_ref[rows, :], v_ref[rows, :]
        s1, s2 = _dot_nt(q1, kj), _dot_nt(q2, kj)
        if masked:
            s1 = jnp.where(keep, s1, NEG_BIG)
            s2 = jnp.where(keep, s2, NEG_BIG)
        m1, l1, a1, m2, l2, a2 = carry
        return update(s1, vj, m1, l1, a1) + update(s2, vj, m2, l2, a2)

    m0 = jnp.full((tq, 1), NEG_BIG, F32)
    l0 = jnp.zeros((tq, 1), F32)
    a0 = jnp.zeros((tq, LANES), F32)
    carry = lax.fori_loop(0, i, lambda j, c: block(j, c, False), (m0, l0, a0, m0, l0, a0))
    m1, l1, a1, m2, l2, a2 = block(i, carry, True)
    lam = _lambda_value(lam_ref, lam_init)
    o = a1 / l1 - lam * (a2 / l2)
    o = o * lax.rsqrt(jnp.mean(o * o, axis=-1, keepdims=True) + RMS_EPS) * w_ref[...] * (1.0 - lam_init)
    o_ref[...] = o.astype(o_ref.dtype)


def attn_prefill(q, k, v, lam_vec, subln_w, *, batch, seq, tq, lam_init):
    n, hd = q.shape
    heads = hd // LANES
    nq = seq // tq
    kv_spec = pl.BlockSpec((seq, LANES), lambda b, h, i: (b, h))
    q_spec = pl.BlockSpec((tq, LANES), lambda b, h, i: (b * nq + i, h))
    return pl.pallas_call(
        functools.partial(_attn_prefill_kernel, tq=tq, lam_init=lam_init),
        out_shape=jax.ShapeDtypeStruct((n, hd), BF16),
        grid=(batch, heads, nq),
        in_specs=[pl.BlockSpec(lam_vec.shape, lambda b, h, i: (0, 0)), q_spec, kv_spec, kv_spec,
                  pl.BlockSpec((1, LANES), lambda b, h, i: (0, 0))],
        out_specs=q_spec,
        compiler_params=_params("parallel", "parallel", "arbitrary"),
        name="attn_prefill",
    )(lam_vec, q, k, v, subln_w.reshape(1, LANES))


def _attn_decode_kernel(pt_ref, lam_ref, q_ref, kn_ref, vn_ref, kc_ref, vc_ref, w_ref, o_ref,
                        m_ref, l_ref, acc_ref, *, lam_init):
    p = pl.program_id(1)
    q = q_ref[0]
    lane = lax.broadcasted_iota(I32, (1, LANES), 1)
    lo = lane < DIFF_HEAD_DIM

    @pl.when(p == 0)
    def _():
        prod = q * kn_ref[0]
        m_ref[0] = jnp.sum(jnp.where(lo, prod, 0.0), axis=-1, keepdims=True)
        m_ref[1] = jnp.sum(jnp.where(lo, 0.0, prod), axis=-1, keepdims=True)
        l_ref[...] = jnp.ones_like(l_ref)
        acc_ref[0] = vn_ref[0]
        acc_ref[1] = vn_ref[0]

    kp, vp = kc_ref[0], vc_ref[0]
    prod = kp * q[None]
    for c in range(2):
        s = jnp.sum(jnp.where(lo, prod, 0.0) if c == 0 else jnp.where(lo, 0.0, prod), axis=-1, keepdims=True)
        m_old = m_ref[c]
        m_new = jnp.maximum(m_old, jnp.max(s, axis=0))
        alpha = jnp.exp(m_old - m_new)
        pr = jnp.exp(s - m_new[None])
        l_ref[c] = alpha * l_ref[c] + jnp.sum(pr, axis=0)
        acc_ref[c] = alpha * acc_ref[c] + jnp.sum(pr * vp, axis=0)
        m_ref[c] = m_new

    @pl.when(p == pl.num_programs(1) - 1)
    def _():
        lam = _lambda_value(lam_ref, lam_init)
        o = acc_ref[0] / l_ref[0] - lam * (acc_ref[1] / l_ref[1])
        o = o * lax.rsqrt(jnp.mean(o * o, axis=-1, keepdims=True) + RMS_EPS) * w_ref[...] * (1.0 - lam_init)
        o_ref[0] = o.astype(o_ref.dtype)


def attn_decode(q, k_new, v_new, cache_k, cache_v, page_table, lam_vec, subln_w, *, lam_init):
    b, heads, _ = q.shape
    n_pages = page_table.shape[1]
    page = cache_k.shape[1]
    tok = pl.BlockSpec((1, heads, LANES), lambda i, p, pt: (i, 0, 0))
    cache = pl.BlockSpec((1, page, heads, LANES), lambda i, p, pt: (pt[i * n_pages + p], 0, 0, 0))
    grid_spec = pltpu.PrefetchScalarGridSpec(
        num_scalar_prefetch=1,
        grid=(b, n_pages),
        in_specs=[pl.BlockSpec(lam_vec.shape, lambda i, p, pt: (0, 0)), tok, tok, tok, cache, cache,
                  pl.BlockSpec((1, LANES), lambda i, p, pt: (0, 0))],
        out_specs=tok,
        scratch_shapes=[pltpu.VMEM((2, heads, 1), F32), pltpu.VMEM((2, heads, 1), F32),
                        pltpu.VMEM((2, heads, LANES), F32)],
    )
    return pl.pallas_call(
        functools.partial(_attn_decode_kernel, lam_init=lam_init),
        out_shape=jax.ShapeDtypeStruct(q.shape, BF16),
        grid_spec=grid_spec,
        compiler_params=_params("parallel", "arbitrary"),
        name="attn_decode",
    )(page_table.reshape(-1), lam_vec, q, k_new, v_new, cache_k, cache_v, subln_w.reshape(1, LANES))


def _router_kernel(x_ref, sc_ref, sh_ref, wr_ref, br_ref, idx_ref, gate_ref, pos_ref, cnt_ref,
                   tri_ref, carry_ref, *, tm):
    i = pl.program_id(0)

    @pl.when(i == 0)
    def _():
        r = lax.broadcasted_iota(I32, (tm, tm), 0)
        c = lax.broadcasted_iota(I32, (tm, tm), 1)
        tri_ref[...] = jnp.where(c < r, 1.0, 0.0).astype(BF16)
        carry_ref[...] = jnp.zeros_like(carry_ref)

    h = x_ref[...] * (1.0 + sc_ref[0]) + sh_ref[0]
    h_hi = h.astype(BF16)
    h_lo = (h - h_hi.astype(F32)).astype(BF16)
    w = wr_ref[...]
    w_hi = w.astype(BF16)
    w_lo = (w - w_hi.astype(F32)).astype(BF16)
    logits = _dot(h_hi, w_hi) + _dot(h_lo, w_hi) + _dot(h_hi, w_lo) + br_ref[...]
    lane = lax.broadcasted_iota(I32, (1, LANES), 1)
    work = jnp.where(lane < N_EXPERTS, logits, NEG_BIG)
    tops, idxs = [], []
    member = jnp.zeros((tm, LANES), F32)
    for _ in range(TOP_K):
        mx = jnp.max(work, axis=1, keepdims=True)
        ix = jnp.min(jnp.where(work == mx, lane, LANES), axis=1, keepdims=True)
        hit = lane == ix
        member = jnp.where(hit, 1.0, member)
        work = jnp.where(hit, NEG_BIG, work)
        tops.append(mx)
        idxs.append(ix)
    ex = [jnp.exp(t - tops[0]) for t in tops]
    denom = ex[0] + ex[1] + ex[2] + ex[3]
    rank = _dot(tri_ref[...], member.astype(BF16)) + carry_ref[...]
    idx_out = jnp.zeros((tm, LANES), I32)
    gate_out = jnp.zeros((tm, LANES), F32)
    pos_out = jnp.zeros((tm, LANES), I32)
    for k in range(TOP_K):
        pk = jnp.sum(jnp.where(lane == idxs[k], rank, 0.0), axis=1, keepdims=True)
        idx_out = jnp.where(lane == k, idxs[k], idx_out)
        gate_out = jnp.where(lane == k, ex[k] / denom, gate_out)
        pos_out = jnp.where(lane == k, pk.astype(I32), pos_out)
    idx_ref[...] = idx_out
    gate_ref[...] = gate_out
    pos_ref[...] = pos_out
    carry_ref[...] = carry_ref[...] + jnp.sum(member, axis=0, keepdims=True)
    cnt_ref[...] = carry_ref[...].astype(I32)


def router(x, sc, sh, w_r, b_r, *, rows_per_group, tm):
    n, d = x.shape
    tiles_per_group = rows_per_group // tm
    r = sc.shape[1]
    w_pad = jnp.zeros((d, LANES), F32).at[:, :N_EXPERTS].set(w_r)
    b_pad = jnp.zeros((1, LANES), F32).at[0, :N_EXPERTS].set(b_r)
    mod = pl.BlockSpec((1, r, d), lambda i: (i // tiles_per_group, 0, 0))
    wide = pl.BlockSpec((tm, LANES), lambda i: (i, 0))
    return pl.pallas_call(
        functools.partial(_router_kernel, tm=tm),
        out_shape=(jax.ShapeDtypeStruct((n, LANES), I32), jax.ShapeDtypeStruct((n, LANES), F32),
                   jax.ShapeDtypeStruct((n, LANES), I32), jax.ShapeDtypeStruct((1, LANES), I32)),
        grid=(n // tm,),
        in_specs=[pl.BlockSpec((tm, d), lambda i: (i, 0)), mod, mod,
                  pl.BlockSpec((d, LANES), lambda i: (0, 0)), pl.BlockSpec((1, LANES), lambda i: (0, 0))],
        out_specs=(wide, wide, wide, pl.BlockSpec((1, LANES), lambda i: (0, 0))),
        scratch_shapes=[pltpu.VMEM((tm, tm), BF16), pltpu.VMEM((1, LANES), F32)],
        compiler_params=_params("arbitrary"),
        name="moe_router",
    )(x, sc, sh, w_pad, b_pad)


def _dispatch_kernel(gap_lo_ref, gap_hi_ref, x_ref, sc_ref, sh_ref, dest_ref, xs_ref,
                     h_ref, zero_ref, idx_ref, sem_ref, isem_ref, *, tm, n_gap_rows):
    i = pl.program_id(0)
    n_idx = tm * TOP_K
    idx_copy = pltpu.make_async_copy(dest_ref.at[0, 0], idx_ref, isem_ref)
    idx_copy.start()
    h_ref[...] = x_ref[...] * (1.0 + sc_ref[0]) + sh_ref[0]
    idx_copy.wait()

    def row_copy(src_row, dst_row):
        return pltpu.make_async_copy(src_row, xs_ref.at[pl.ds(dst_row, 1)], sem_ref)

    def issue(r, carry):
        for k in range(TOP_K):
            row_copy(h_ref.at[pl.ds(r, 1)], idx_ref[r * TOP_K + k]).start()
        return carry

    lax.fori_loop(0, tm, issue, 0)

    def drain(r, carry):
        row_copy(h_ref.at[pl.ds(0, 1)], 0).wait()
        return carry

    lax.fori_loop(0, n_idx, drain, 0)

    @pl.when(i == 0)
    def _():
        zero_ref[...] = jnp.zeros_like(zero_ref)

        def per_expert(e, carry):
            def fill(rw, c2):
                row_copy(zero_ref.at[pl.ds(0, 1)], rw).start()
                return c2
            return lax.fori_loop(gap_lo_ref[e], gap_hi_ref[e], fill, carry)

        lax.fori_loop(0, N_EXPERTS, per_expert, 0)

        def drain_zero(r, carry):
            row_copy(zero_ref.at[pl.ds(0, 1)], 0).wait()
            return carry

        lax.fori_loop(0, n_gap_rows, drain_zero, 0)


def dispatch(x, sc, sh, dest, gap_lo, gap_hi, *, rows_per_group, tm, n_slots_padded):
    n, d = x.shape
    tiles_per_group = rows_per_group // tm
    r = sc.shape[1]
    n_idx = tm * TOP_K
    mod = pl.BlockSpec((1, r, d), lambda i, lo, hi: (i // tiles_per_group, 0, 0))
    grid_spec = pltpu.PrefetchScalarGridSpec(
        num_scalar_prefetch=2,
        grid=(n // tm,),
        in_specs=[pl.BlockSpec((tm, d), lambda i, lo, hi: (i, 0)), mod, mod,
                  pl.BlockSpec((1, 1, n_idx), lambda i, lo, hi: (i, 0, 0))],
        out_specs=pl.BlockSpec(memory_space=pl.ANY),
        scratch_shapes=[pltpu.VMEM((tm, d), F32), pltpu.VMEM((8, d), F32), pltpu.SMEM((n_idx,), I32),
                        pltpu.SemaphoreType.DMA, pltpu.SemaphoreType.DMA],
    )
    return pl.pallas_call(
        functools.partial(_dispatch_kernel, tm=tm, n_gap_rows=n_slots_padded - n * TOP_K),
        out_shape=jax.ShapeDtypeStruct((n_slots_padded, d), F32),
        grid_spec=grid_spec,
        compiler_params=_params("arbitrary"),
        name="moe_dispatch",
    )(gap_lo, gap_hi, x, sc, sh, dest.reshape(n // tm, 1, n_idx))


def _expert_ffn_kernel(be_ref, xs_ref, wg_ref, bg_ref, wu_ref, bu_ref, wd_ref, bd_ref, ys_ref):
    x = xs_ref[...].astype(BF16)
    gate = jnp.minimum(_dot(x, wg_ref[0]) + bg_ref[0], SWIGLU_LIMIT)
    up = jnp.clip(_dot(x, wu_ref[0]) + bu_ref[0], -SWIGLU_LIMIT, SWIGLU_LIMIT)
    act = (up + 1.0) * (gate * _sigmoid(gate * SWIGLU_ALPHA))
    ys_ref[...] = _dot(act.astype(BF16), wd_ref[0]) + bd_ref[0]


def expert_ffn(xs, block_expert, wg, bg, wu, bu, wd, bd, *, blk):
    ns, d = xs.shape
    de = wg.shape[2]
    rows = pl.BlockSpec((blk, d), lambda i, be: (i, 0))
    wspec = lambda a, b: pl.BlockSpec((1, a, b), lambda i, be: (be[i], 0, 0))
    grid_spec = pltpu.PrefetchScalarGridSpec(
        num_scalar_prefetch=1,
        grid=(ns // blk,),
        in_specs=[rows, wspec(d, de), wspec(1, de), wspec(d, de), wspec(1, de), wspec(de, d), wspec(1, d)],
        out_specs=rows,
    )
    return pl.pallas_call(
        _expert_ffn_kernel,
        out_shape=jax.ShapeDtypeStruct((ns, d), F32),
        grid_spec=grid_spec,
        compiler_params=_params("arbitrary"),
        name="moe_expert_ffn",
    )(block_expert, xs, wg, bg, wu, bu, wd, bd)


def _combine_ln_kernel(x_ref, gm_ref, gate_ref, dest_ref, ys_ref, lg_ref, lb_ref, o_ref,
                       buf_ref, idx_ref, sem_ref, isem_ref, *, tm, alpha):
    n_idx = tm * TOP_K
    idx_copy = pltpu.make_async_copy(dest_ref.at[0, 0], idx_ref, isem_ref)
    idx_copy.start()
    idx_copy.wait()

    def row_copy(src_row, k, r):
        return pltpu.make_async_copy(ys_ref.at[pl.ds(src_row, 1)], buf_ref.at[k, pl.ds(r, 1)], sem_ref)

    def issue(r, carry):
        for k in range(TOP_K):
            row_copy(idx_ref[r * TOP_K + k], k, r).start()
        return carry

    lax.fori_loop(0, tm, issue, 0)

    def drain(r, carry):
        row_copy(0, 0, 0).wait()
        return carry

    lax.fori_loop(0, n_idx, drain, 0)
    gates = gate_ref[...]
    y = gates[:, 0:1] * buf_ref[0]
    for k in range(1, TOP_K):
        y = y + gates[:, k:k + 1] * buf_ref[k]
    o_ref[...] = _layer_norm(alpha * x_ref[...] + gm_ref[0] * y, lg_ref[...], lb_ref[...])


def combine_ln(x, gm, gates, dest, ys, ln_g, ln_b, *, rows_per_group, tm, alpha):
    n, d = x.shape
    tiles_per_group = rows_per_group // tm
    r = gm.shape[1]
    n_idx = tm * TOP_K
    row = pl.BlockSpec((tm, d), lambda i: (i, 0))
    vec = pl.BlockSpec((1, d), lambda i: (0, 0))
    return pl.pallas_call(
        functools.partial(_combine_ln_kernel, tm=tm, alpha=alpha),
        out_shape=jax.ShapeDtypeStruct((n, d), F32),
        grid=(n // tm,),
        in_specs=[row, pl.BlockSpec((1, r, d), lambda i: (i // tiles_per_group, 0, 0)),
                  pl.BlockSpec((tm, LANES), lambda i: (i, 0)),
                  pl.BlockSpec((1, 1, n_idx), lambda i: (i, 0, 0)),
                  pl.BlockSpec(memory_space=pl.ANY), vec, vec],
        out_specs=row,
        scratch_shapes=[pltpu.VMEM((TOP_K, tm, d), F32), pltpu.SMEM((n_idx,), I32),
                        pltpu.SemaphoreType.DMA, pltpu.SemaphoreType.DMA],
        compiler_params=_params("arbitrary"),
        name="moe_combine_ln",
    )(x, gm, gates, dest.reshape(n // tm, 1, n_idx), ys, ln_g.reshape(1, d), ln_b.reshape(1, d))


def _moe_layer(x, sc, sh, gm, ln_g, ln_b, w_r, b_r, wg, bg, wu, bu, wd, bd, *, rows_per_group, tm, blk, alpha):
    n, d = x.shape
    idx_w, gate_w, pos_w, cnt_w = router(x, sc, sh, w_r, b_r, rows_per_group=rows_per_group, tm=tm)
    counts = cnt_w[0, :N_EXPERTS]
    top_idx, pos = idx_w[:, :TOP_K], pos_w[:, :TOP_K]
    n_slots = n * TOP_K
    n_blocks = -(-(n_slots + N_EXPERTS * (blk - 1)) // blk)
    padded = (counts + blk - 1) // blk * blk
    pad_end = jnp.cumsum(padded)
    pad_start = pad_end - padded
    onehot = top_idx[..., None] == jnp.arange(N_EXPERTS, dtype=I32)
    dest = (jnp.sum(jnp.where(onehot, pad_start, 0), axis=-1) + pos).astype(I32)
    block_expert = jnp.minimum(
        jnp.searchsorted(pad_end, jnp.arange(n_blocks, dtype=I32) * blk, side="right"), N_EXPERTS - 1).astype(I32)
    gap_lo = (pad_start + counts).astype(I32)
    gap_hi = jnp.concatenate([pad_start[1:], jnp.full((1,), n_blocks * blk, I32)]).astype(I32)
    xs = dispatch(x, sc, sh, dest, gap_lo, gap_hi, rows_per_group=rows_per_group, tm=tm,
                  n_slots_padded=n_blocks * blk)
    ys = expert_ffn(xs, block_expert, wg, bg, wu, bu, wd, bd, blk=blk)
    return combine_ln(x, gm, gate_w, dest, ys, ln_g, ln_b, rows_per_group=rows_per_group, tm=tm, alpha=alpha)


def _rope_tables(pos):
    inv_freq = ROPE_THETA ** (-jnp.arange(ROPE_HALF, dtype=F32) / ROPE_HALF)
    ang = pos.astype(F32)[:, None] * inv_freq[None, :]
    cos, sin = jnp.cos(ang), jnp.sin(ang)
    reps = LANES // DIFF_HEAD_DIM
    return (jnp.concatenate([cos, cos] * reps, axis=1), jnp.concatenate([-sin, sin] * reps, axis=1))


def _trunk(x, mods, kv_mods, past_len, s0, cache, wts, *, batch, seq, tm, blk):
    n, d = x.shape
    depth = wts["ada_w"].shape[0]
    n_a = wts["hgrn_w_in"].shape[0]
    alpha = (2 * depth) ** 0.25
    rows_per_group = seq if seq >= tm else n
    kw = dict(rows_per_group=rows_per_group, tm=tm)
    decode = seq == 1
    pos = past_len + (jnp.zeros((n,), I32) if decode else jnp.arange(seq, dtype=I32))
    rope = _rope_tables(pos)
    heads = d // LANES
    states = []
    k_new = v_new = k16 = v16 = None
    for l in range(depth):
        sh_m, sc_m, g_m, sh_f, sc_f, g_f = mods[l]
        if l < n_a:
            proj, = modmm(x, sc_m, sh_m, wts["hgrn_w_in"][l], out_dtypes=(F32,), **kw)
            lbs = [a[l][None] for a in wts["lb"]]
            nw = wts["hgrn_norm_w"][l].reshape(1, LANES)
            if decode:
                o, s_l = hgrn_step(proj, *lbs, nw, s0[l])
            else:
                o, s_l = hgrn_chunked(proj, *lbs, nw, s0[l], batch=batch, seq=seq, tb=min(seq, 512))
            states.append(s_l)
            w_out = wts["hgrn_w_out"][l]
        else:
            j = l - n_a
            lam_init = 0.8 - 0.6 * math.exp(-0.3 * l)
            if l == n_a:
                kv_shift, kv_scale = kv_mods
                half = wts["kv_w"].shape[1] // 2
                k_new, k16 = modmm(x, kv_scale, kv_shift, wts["kv_w"][:, :half], out_dtypes=(F32, BF16),
                                   rope=rope, **kw)
                v_new, v16 = modmm(x, kv_scale, kv_shift, wts["kv_w"][:, half:], out_dtypes=(F32, BF16), **kw)
            q_scale = DIFF_HEAD_DIM ** -0.5
            if decode:
                q, = modmm(x, sc_m, sh_m, wts["diff_w_q"][j], out_dtypes=(F32,), rope=rope, out_scale=q_scale, **kw)
                o = attn_decode(q.reshape(n, heads, LANES), k_new.reshape(n, heads, LANES),
                                v_new.reshape(n, heads, LANES), cache[0], cache[1], cache[2],
                                wts["diff_lambda"][j], wts["diff_subln_w"][j], lam_init=lam_init)
                o = o.reshape(n, d)
            else:
                q, = modmm(x, sc_m, sh_m, wts["diff_w_q"][j], out_dtypes=(BF16,), rope=rope, out_scale=q_scale, **kw)
                o = attn_prefill(q, k16, v16, wts["diff_lambda"][j], wts["diff_subln_w"][j],
                                 batch=batch, seq=seq, tq=min(seq, 512), lam_init=lam_init)
            w_out = wts["diff_w_out"][j]
        x = proj_ln(o, w_out, x, g_m, wts["ln_g"][l, 0], wts["ln_b"][l, 0], alpha=alpha, **kw)
        x = _moe_layer(x, sc_f, sh_f, g_f, wts["ln_g"][l, 1], wts["ln_b"][l, 1],
                       wts["moe_w_router"][l], wts["moe_b_router"][l],
                       wts["moe_w_gate"][l], wts["moe_b_gate"][l], wts["moe_w_up"][l], wts["moe_b_up"][l],
                       wts["moe_w_down"][l], wts["moe_b_down"][l], alpha=alpha, blk=blk, **kw)
    return x, k_new, v_new, jnp.stack(states)


def kernel(x_prompt, x_sample, cache_k, cache_v, state_hgrn, page_table, c_prompt, c_sample, ada_w, ada_b, ln_g, ln_b, hgrn_w_in, hgrn_lb_logits, hgrn_norm_w, hgrn_w_out, kv_ada_w, kv_ada_b, kv_w, diff_w_q, diff_lambda, diff_subln_w, diff_w_out, moe_w_router, moe_b_router, moe_w_gu, moe_b_gu, moe_w_down, moe_b_down):
    b_p, t_p, d = x_prompt.shape
    b_d, t_d, _ = x_sample.shape
    depth = ada_w.shape[0]
    n_a = hgrn_w_in.shape[0]
    heads = d // LANES
    de = moe_w_down.shape[2]

    m_rows = -(-(b_d + b_p) // 8) * 8
    c_all = jnp.zeros((m_rows, d), F32).at[:b_d].set(c_sample).at[b_d:b_d + b_p].set(c_prompt)
    mod_all = adaln(c_all, ada_w, ada_b)
    kv_mod_all = adaln(c_all, kv_ada_w[None], kv_ada_b[None])[0]

    def split_mods(m, n_vec, lo, hi, per_row):
        parts = [m[lo:hi, v * d:(v + 1) * d] for v in range(n_vec)]
        return [p[None] if per_row else p[:, None, :] for p in parts]

    mods_p = [split_mods(mod_all[l], 6, b_d, b_d + b_p, False) for l in range(depth)]
    mods_s = [split_mods(mod_all[l], 6, 0, b_d, True) for l in range(depth)]
    kv_mods_p = split_mods(kv_mod_all, 2, b_d, b_d + b_p, False)
    kv_mods_s = split_mods(kv_mod_all, 2, 0, b_d, True)

    loglb, log1m, onem = hgrn_lower_bounds(hgrn_lb_logits)

    wts = dict(
        ada_w=ada_w, ln_g=ln_g, ln_b=ln_b,
        hgrn_w_in=hgrn_w_in.astype(BF16), lb=(loglb, log1m, onem), hgrn_norm_w=hgrn_norm_w,
        hgrn_w_out=hgrn_w_out.astype(BF16), kv_w=kv_w.astype(BF16), diff_w_q=diff_w_q.astype(BF16),
        diff_lambda=diff_lambda, diff_subln_w=diff_subln_w, diff_w_out=diff_w_out.astype(BF16),
        moe_w_router=moe_w_router, moe_b_router=moe_b_router,
        moe_w_gate=moe_w_gu[..., 0::2].astype(BF16), moe_w_up=moe_w_gu[..., 1::2].astype(BF16),
        moe_b_gate=moe_b_gu[..., 0::2].reshape(depth, N_EXPERTS, 1, de),
        moe_b_up=moe_b_gu[..., 1::2].reshape(depth, N_EXPERTS, 1, de),
        moe_w_down=moe_w_down.astype(BF16), moe_b_down=moe_b_down.reshape(depth, N_EXPERTS, 1, d),
    )

    s0_prompt = jnp.zeros((n_a, b_p, heads, LANES, LANES), F32)
    y_p, k_p, v_p, st_p = _trunk(x_prompt.reshape(b_p * t_p, d), mods_p, kv_mods_p, 0, s0_prompt, None, wts,
                                 batch=b_p, seq=t_p, tm=min(512, t_p), blk=256)
    past_len = page_table.shape[1] * cache_k.shape[1]
    y_s, k_s, v_s, st_s = _trunk(x_sample.reshape(b_d * t_d, d), mods_s, kv_mods_s, past_len, state_hgrn,
                                 (cache_k, cache_v, page_table), wts, batch=b_d, seq=t_d, tm=b_d, blk=16)
    dv = d // heads
    return (y_p.reshape(b_p, t_p, d), y_s.reshape(b_d, t_d, d),
            k_p.reshape(b_p, t_p, heads, dv), v_p.reshape(b_p, t_p, heads, dv), st_p,
            k_s.reshape(b_d, t_d, heads, dv), v_s.reshape(b_d, t_d, heads, dv), st_s)
```

```python
import functools
import math

import jax
import jax.numpy as jnp
from jax import lax
from jax.experimental import pallas as pl
from jax.experimental.pallas import tpu as pltpu

F32 = jnp.float32
BF16 = jnp.bfloat16
I32 = jnp.int32

LANES = 128
HGRN_CHUNK = 64
HGRN_SUB = 16
LB_FLOOR = 1e-30
DIFF_HEAD_DIM = 64
ROPE_THETA = 10000.0
ROPE_HALF = DIFF_HEAD_DIM // 2
NEG_BIG = -1e30
N_EXPERTS = 32
TOP_K = 4
SWIGLU_LIMIT = 7.0
SWIGLU_ALPHA = 1.702
LN_EPS = 1e-5
RMS_EPS = 1e-5
VMEM_LIMIT = 56 * 1024 * 1024
DECODE_PAGES_PER_STEP = 8
GU_GROUP = 256
DMA_UNROLL = 8


def _params(*sem):
    return pltpu.CompilerParams(dimension_semantics=sem, vmem_limit_bytes=VMEM_LIMIT)


def _sigmoid(x):
    return 1.0 / (1.0 + jnp.exp(-x))


def _silu(x):
    return x * _sigmoid(x)


def _log_sigmoid(x):
    return jnp.minimum(x, 0.0) - jnp.log1p(jnp.exp(-jnp.abs(x)))


def _layer_norm(z, g, b):
    mu = jnp.mean(z, axis=-1, keepdims=True)
    zc = z - mu
    var = jnp.mean(zc * zc, axis=-1, keepdims=True)
    return zc * lax.rsqrt(var + LN_EPS) * g + b


def _dot(a, b):
    return jnp.dot(a, b, preferred_element_type=F32)


def _dot_nt(a, b):
    return lax.dot_general(a, b, (((1,), (1,)), ((), ())), preferred_element_type=F32)


def _dot_tn(a, b):
    return lax.dot_general(a, b, (((0,), (0,)), ((), ())), preferred_element_type=F32)


def _split3(x):
    hi = x.astype(BF16)
    r1 = x - hi.astype(F32)
    mid = r1.astype(BF16)
    lo = (r1 - mid.astype(F32)).astype(BF16)
    return hi, mid, lo


def _adaln_kernel(c_ref, w_ref, b_ref, o_ref):
    s = _silu(c_ref[...]).astype(BF16)
    o_ref[0] = _dot(s, w_ref[0].astype(BF16)) + b_ref[0]


def adaln(c, w, b, tn=1024):
    m, d = c.shape
    nl, _, no = w.shape
    return pl.pallas_call(
        _adaln_kernel,
        out_shape=jax.ShapeDtypeStruct((nl, m, no), F32),
        grid=(nl, no // tn),
        in_specs=[
            pl.BlockSpec((m, d), lambda l, j: (0, 0)),
            pl.BlockSpec((1, d, tn), lambda l, j: (l, 0, j)),
            pl.BlockSpec((1, 1, tn), lambda l, j: (l, 0, j)),
        ],
        out_specs=pl.BlockSpec((1, m, tn), lambda l, j: (l, 0, j)),
        compiler_params=_params("parallel", "parallel"),
        name="adaln",
    )(c, w, b.reshape(nl, 1, no))


def _lb_kernel(logit_ref, loglb_ref, log1m_ref, onem_ref):
    x = logit_ref[...]
    nl = x.shape[0]
    e = jnp.exp(x - jnp.max(x, axis=0, keepdims=True))
    sm = e / jnp.sum(e, axis=0, keepdims=True)
    acc = jnp.zeros_like(sm[0:1])
    for l in range(nl):
        lb = jnp.clip(acc, 0.0, 1.0 - 1e-6)
        loglb_ref[l] = jnp.log(jnp.maximum(lb, LB_FLOOR))
        log1m_ref[l] = jnp.log1p(-lb)
        onem_ref[l] = 1.0 - lb
        acc = acc + sm[l:l + 1]


def hgrn_lower_bounds(lb_logits):
    nl, f = lb_logits.shape
    shp = jax.ShapeDtypeStruct((nl, 1, f), F32)
    return pl.pallas_call(_lb_kernel, out_shape=(shp, shp, shp), name="hgrn_lb")(lb_logits)


def _rotary_tile(y, cos, sin_signed):
    lane = lax.broadcasted_iota(I32, (1, LANES), 1)
    first_half = (lane % DIFF_HEAD_DIM) < ROPE_HALF
    partner = jnp.where(first_half, pltpu.roll(y, LANES - ROPE_HALF, 1), pltpu.roll(y, ROPE_HALF, 1))
    return y * cos + partner * sin_signed


def _modmm_kernel(*refs, rotary, out_scale, n_out):
    if rotary:
        x_ref, sc_ref, sh_ref, w_ref, cos_ref, sin_ref = refs[:6]
        rest = refs[6:]
    else:
        x_ref, sc_ref, sh_ref, w_ref = refs[:4]
        rest = refs[4:]
    out_refs, h_ref = rest[:n_out], rest[n_out]

    @pl.when(pl.program_id(1) == 0)
    def _():
        h_ref[...] = (x_ref[...] * (1.0 + sc_ref[0]) + sh_ref[0]).astype(BF16)

    y = _dot(h_ref[...], w_ref[...])
    if rotary:
        cos, sin = cos_ref[...], sin_ref[...]
        y = jnp.concatenate(
            [_rotary_tile(y[:, g * LANES:(g + 1) * LANES], cos, sin) for g in range(y.shape[1] // LANES)], axis=1)
    if out_scale != 1.0:
        y = y * out_scale
    for o_ref in out_refs:
        o_ref[...] = y.astype(o_ref.dtype)


def modmm(x, sc, sh, w, *, rows_per_group, tm, out_dtypes, rope=None, out_scale=1.0, tn=1024):
    n, d = x.shape
    no = w.shape[1]
    tiles_per_group = rows_per_group // tm
    r = sc.shape[1]
    in_specs = [
        pl.BlockSpec((tm, d), lambda i, j: (i, 0)),
        pl.BlockSpec((1, r, d), lambda i, j: (i // tiles_per_group, 0, 0)),
        pl.BlockSpec((1, r, d), lambda i, j: (i // tiles_per_group, 0, 0)),
        pl.BlockSpec((d, tn), lambda i, j: (0, j)),
    ]
    args = [x, sc, sh, w]
    if rope is not None:
        pos_tiles = rope[0].shape[0] // tm
        in_specs += [pl.BlockSpec((tm, LANES), lambda i, j: (i % pos_tiles, 0))] * 2
        args += list(rope)
    outs = tuple(jax.ShapeDtypeStruct((n, no), dt) for dt in out_dtypes)
    res = pl.pallas_call(
        functools.partial(_modmm_kernel, rotary=rope is not None, out_scale=out_scale, n_out=len(outs)),
        out_shape=outs,
        grid=(n // tm, no // tn),
        in_specs=in_specs,
        out_specs=tuple(pl.BlockSpec((tm, tn), lambda i, j: (i, j)) for _ in outs),
        scratch_shapes=[pltpu.VMEM((tm, d), BF16)],
        compiler_params=_params("parallel", "arbitrary"),
        name="modmm",
    )(*args)
    return res


def _hgrn_head_chunk(q_raw, f, v, g_raw, loglb, log1m, onem, nw, st, tri, sub_row):
    c, sub = HGRN_CHUNK, HGRN_SUB
    q = _silu(q_raw)
    x2 = log1m + _log_sigmoid(f)
    mx = jnp.maximum(loglb, x2)
    logf = mx + jnp.log1p(jnp.exp(jnp.minimum(loglb, x2) - mx))
    kk = onem * _sigmoid(-f)
    hi, mid, lo = _split3(logf)
    b = _dot(tri, hi) + _dot(tri, mid) + _dot(tri, lo)
    v16 = v.astype(BF16)
    o_inter = _dot_nt((q * jnp.exp(b)).astype(BF16), st.astype(BF16))
    o_parts = []
    for i in range(c // sub):
        sl = slice(i * sub, (i + 1) * sub)
        qi, bi, ki, vi = q[sl], b[sl], kk[sl], v[sl]
        oi = o_inter[sl]
        for s in range(sub):
            w = qi * ki[s:s + 1] * jnp.exp(jnp.minimum(bi - bi[s:s + 1], 0.0))
            a = jnp.sum(w, axis=1, keepdims=True)
            a = jnp.where(sub_row >= s, a, 0.0)
            oi = oi + a * vi[s:s + 1]
        if i > 0:
            r = b[i * sub - 1:i * sub]
            qs = (qi * jnp.exp(bi - r)).astype(BF16)
            ks = (kk[:i * sub] * jnp.exp(r - b[:i * sub])).astype(BF16)
            a = _dot_nt(qs, ks)
            oi = oi + _dot(a.astype(BF16), v16[:i * sub])
        o_parts.append(oi)
    o = jnp.concatenate(o_parts, axis=0)
    b_last = b[c - 1:c]
    kd = (kk * jnp.exp(b_last - b)).astype(BF16)
    st_new = st * jnp.exp(b_last) + _dot_tn(v16, kd)
    o = o * lax.rsqrt(jnp.mean(o * o, axis=-1, keepdims=True) + RMS_EPS) * nw * _silu(g_raw)
    return o, st_new


def _hgrn_chunk_kernel(q_ref, f_ref, v_ref, g_ref, loglb_ref, log1m_ref, onem_ref, nw_ref, s0_ref,
                       o_ref, sfin_ref, st_ref, *, n_chunks, hg):
    t = pl.program_id(2)
    c = HGRN_CHUNK

    @pl.when(t == 0)
    def _():
        for hh in range(hg):
            st_ref[hh] = s0_ref[0, hh].T

    loglb, log1m, onem, nw = loglb_ref[0], log1m_ref[0], onem_ref[0], nw_ref[...]
    row = lax.broadcasted_iota(I32, (c, c), 0)
    col = lax.broadcasted_iota(I32, (c, c), 1)
    tri = jnp.where(col <= row, 1.0, 0.0).astype(BF16)
    sub_row = lax.broadcasted_iota(I32, (HGRN_SUB, 1), 0)

    def chunk(ci, carry):
        rows = pl.ds(pl.multiple_of(ci * c, c), c)
        outs = []
        for hh in range(hg):
            ln = slice(hh * LANES, (hh + 1) * LANES)
            o, st_new = _hgrn_head_chunk(q_ref[rows, ln], f_ref[rows, ln], v_ref[rows, ln], g_ref[rows, ln],
                                         loglb[:, ln], log1m[:, ln], onem[:, ln], nw, st_ref[hh], tri, sub_row)
            st_ref[hh] = st_new
            outs.append(o)
        o_ref[rows, :] = jnp.concatenate(outs, axis=1).astype(o_ref.dtype)
        return carry

    lax.fori_loop(0, n_chunks, chunk, 0)

    @pl.when(t == pl.num_programs(2) - 1)
    def _():
        for hh in range(hg):
            sfin_ref[0, hh] = st_ref[hh].T


def hgrn_chunked(proj, loglb, log1m, onem, norm_w, s0, *, batch, seq, tb, hg=4):
    n = proj.shape[0]
    heads = proj.shape[1] // (4 * LANES)
    tpb = seq // tb
    ngrp = heads // hg
    w = hg * LANES
    row_blk = lambda off: pl.BlockSpec((tb, w), lambda b, h, t: (b * tpb + t, off * ngrp + h))
    vec = pl.BlockSpec((1, 1, w), lambda b, h, t: (0, 0, h))
    st_spec = pl.BlockSpec((1, hg, LANES, LANES), lambda b, h, t: (b, h, 0, 0))
    return pl.pallas_call(
        functools.partial(_hgrn_chunk_kernel, n_chunks=tb // HGRN_CHUNK, hg=hg),
        out_shape=(jax.ShapeDtypeStruct((n, heads * LANES), BF16),
                   jax.ShapeDtypeStruct(s0.shape, F32)),
        grid=(batch, ngrp, tpb),
        in_specs=[row_blk(0), row_blk(1), row_blk(2), row_blk(3), vec, vec, vec,
                  pl.BlockSpec((1, LANES), lambda b, h, t: (0, 0)), st_spec],
        out_specs=(pl.BlockSpec((tb, w), lambda b, h, t: (b * tpb + t, h)), st_spec),
        scratch_shapes=[pltpu.VMEM((hg, LANES, LANES), F32)],
        compiler_params=_params("parallel", "parallel", "arbitrary"),
        name="hgrn_chunked",
    )(proj, proj, proj, proj, loglb, log1m, onem, norm_w, s0)


def _hgrn_step_kernel(q_ref, f_ref, v_ref, g_ref, loglb_ref, log1m_ref, onem_ref, nw_ref, s_ref,
                      o_ref, snew_ref, *, bb):
    f = f_ref[...]
    q = _silu(q_ref[...])
    v = v_ref[...]
    x1 = loglb_ref[0]
    x2 = log1m_ref[0] + _log_sigmoid(f)
    mx = jnp.maximum(x1, x2)
    decay = jnp.exp(mx + jnp.log1p(jnp.exp(jnp.minimum(x1, x2) - mx)))
    kk = onem_ref[0] * _sigmoid(-f)
    q_t, d_t, k_t = q.T, decay.T, kk.T
    rows = []
    for e in range(bb):
        s_new = s_ref[e, 0] * d_t[:, e:e + 1] + k_t[:, e:e + 1] * v[e:e + 1, :]
        snew_ref[e, 0] = s_new
        rows.append(jnp.sum(q_t[:, e:e + 1] * s_new, axis=0, keepdims=True))
    o = jnp.concatenate(rows, axis=0)
    o = o * lax.rsqrt(jnp.mean(o * o, axis=-1, keepdims=True) + RMS_EPS) * nw_ref[...] * _silu(g_ref[...])
    o_ref[...] = o.astype(o_ref.dtype)


def hgrn_step(proj, loglb, log1m, onem, norm_w, state, *, bb=16):
    n = proj.shape[0]
    heads = proj.shape[1] // (4 * LANES)
    row_blk = lambda off: pl.BlockSpec((bb, LANES), lambda i, h: (i, off * heads + h))
    vec = pl.BlockSpec((1, 1, LANES), lambda i, h: (0, 0, h))
    st_spec = pl.BlockSpec((bb, 1, LANES, LANES), lambda i, h: (i, h, 0, 0))
    return pl.pallas_call(
        functools.partial(_hgrn_step_kernel, bb=bb),
        out_shape=(jax.ShapeDtypeStruct((n, heads * LANES), BF16),
                   jax.ShapeDtypeStruct(state.shape, F32)),
        grid=(n // bb, heads),
        in_specs=[row_blk(0), row_blk(1), row_blk(2), row_blk(3), vec, vec, vec,
                  pl.BlockSpec((1, LANES), lambda i, h: (0, 0)), st_spec],
        out_specs=(pl.BlockSpec((bb, LANES), lambda i, h: (i, h)), st_spec),
        compiler_params=_params("parallel", "parallel"),
        name="hgrn_step",
    )(proj, proj, proj, proj, loglb, log1m, onem, norm_w, state)


def _proj_ln_kernel(a_ref, w_ref, x_ref, gm_ref, lg_ref, lb_ref, o_ref, *, alpha):
    y = _dot(a_ref[...], w_ref[...])
    o_ref[...] = _layer_norm(alpha * x_ref[...] + gm_ref[0] * y, lg_ref[...], lb_ref[...])


def proj_ln(a, w, x, gm, ln_g, ln_b, *, rows_per_group, tm, alpha):
    n, d = x.shape
    tiles_per_group = rows_per_group // tm
    r = gm.shape[1]
    row = pl.BlockSpec((tm, d), lambda i: (i, 0))
    vec = pl.BlockSpec((1, d), lambda i: (0, 0))
    return pl.pallas_call(
        functools.partial(_proj_ln_kernel, alpha=alpha),
        out_shape=jax.ShapeDtypeStruct((n, d), F32),
        grid=(n // tm,),
        in_specs=[row, pl.BlockSpec((d, d), lambda i: (0, 0)), row,
                  pl.BlockSpec((1, r, d), lambda i: (i // tiles_per_group, 0, 0)), vec, vec],
        out_specs=row,
        compiler_params=_params("parallel"),
        name="proj_ln",
    )(a, w, x, gm, ln_g.reshape(1, d), ln_b.reshape(1, d))


def _lambda_value(lam_ref, lam_init):
    lv = lam_ref[...]
    s1 = jnp.sum(lv[0:1] * lv[1:2], axis=1, keepdims=True)
    s2 = jnp.sum(lv[2:3] * lv[3:4], axis=1, keepdims=True)
    return jnp.exp(s1) - jnp.exp(s2) + lam_init


def _attn_prefill_kernel(lam_ref, q_ref, k_ref, v_ref, w_ref, o_ref, *, tq, lam_init):
    i = pl.program_id(2)
    q = q_ref[...]
    lane = lax.broadcasted_iota(I32, (1, LANES), 1)
    zero = jnp.zeros_like(q)
    q1 = jnp.where(lane < DIFF_HEAD_DIM, q, zero)
    q2 = jnp.where(lane < DIFF_HEAD_DIM, zero, q)
    keep = lax.broadcasted_iota(I32, (tq, tq), 1) <= lax.broadcasted_iota(I32, (tq, tq), 0)

    def update(s, vj, m, l, acc):
        m_new = jnp.maximum(m, jnp.max(s, axis=1, keepdims=True))
        alpha = jnp.exp(m - m_new)
        p = jnp.exp(s - m_new)
        l = alpha * l + jnp.sum(p, axis=1, keepdims=True)
        acc = alpha * acc + _dot(p.astype(BF16), vj)
        return m_new, l, acc

    def block(j, carry, masked):
        rows = pl.ds(pl.multiple_of(j * tq, tq), tq)
        kj, vj = k_ref[rows, :], v_ref[rows, :]
        s1, s2 = _dot_nt(q1, kj), _dot_nt(q2, kj)
        if masked:
            s1 = jnp.where(keep, s1, NEG_BIG)
            s2 = jnp.where(keep, s2, NEG_BIG)
        m1, l1, a1, m2, l2, a2 = carry
        return update(s1, vj, m1, l1, a1) + update(s2, vj, m2, l2, a2)

    m0 = jnp.full((tq, 1), NEG_BIG, F32)
    l0 = jnp.zeros((tq, 1), F32)
    a0 = jnp.zeros((tq, LANES), F32)
    carry = lax.fori_loop(0, i, lambda j, c: block(j, c, False), (m0, l0, a0, m0, l0, a0))
    m1, l1, a1, m2, l2, a2 = block(i, carry, True)
    lam = _lambda_value(lam_ref, lam_init)
    o = a1 / l1 - lam * (a2 / l2)
    o = o * lax.rsqrt(jnp.mean(o * o, axis=-1, keepdims=True) + RMS_EPS) * w_ref[...] * (1.0 - lam_init)
    o_ref[...] = o.astype(o_ref.dtype)


def attn_prefill(q, k, v, lam_vec, subln_w, *, batch, seq, tq, lam_init):
    n, hd = q.shape
    heads = hd // LANES
    nq = seq // tq
    kv_spec = pl.BlockSpec((seq, LANES), lambda b, h, i: (b, h))
    q_spec = pl.BlockSpec((tq, LANES), lambda b, h, i: (b * nq + i, h))
    return pl.pallas_call(
        functools.partial(_attn_prefill_kernel, tq=tq, lam_init=lam_init),
        out_shape=jax.ShapeDtypeStruct((n, hd), BF16),
        grid=(batch, heads, nq),
        in_specs=[pl.BlockSpec(lam_vec.shape, lambda b, h, i: (0, 0)), q_spec, kv_spec, kv_spec,
                  pl.BlockSpec((1, LANES), lambda b, h, i: (0, 0))],
        out_specs=q_spec,
        compiler_params=_params("parallel", "parallel", "arbitrary"),
        name="attn_prefill",
    )(lam_vec, q, k, v, subln_w.reshape(1, LANES))


def _attn_decode_kernel(pt_ref, lam_ref, q_ref, kn_ref, vn_ref, *rest, lam_init, pg):
    kc_refs, vc_refs = rest[:pg], rest[pg:2 * pg]
    w_ref, o_ref, m_ref, l_ref, acc_ref = rest[2 * pg:]
    p = pl.program_id(1)
    q = q_ref[0]
    lane = lax.broadcasted_iota(I32, (1, LANES), 1)
    lo = lane < DIFF_HEAD_DIM
    q_c = (jnp.where(lo, q, 0.0), jnp.where(lo, 0.0, q))

    @pl.when(p == 0)
    def _():
        for c in range(2):
            m_ref[c] = jnp.sum(q_c[c] * kn_ref[0], axis=-1, keepdims=True)
            acc_ref[c] = vn_ref[0]
        l_ref[...] = jnp.ones_like(l_ref)

    for c in range(2):
        s = [jnp.sum(kc_refs[j][0] * q_c[c][None], axis=-1, keepdims=True) for j in range(pg)]
        m_old = m_ref[c]
        m_new = m_old
        for j in range(pg):
            m_new = jnp.maximum(m_new, jnp.max(s[j], axis=0))
        alpha = jnp.exp(m_old - m_new)
        l_new = alpha * l_ref[c]
        acc = alpha * acc_ref[c]
        for j in range(pg):
            pr = jnp.exp(s[j] - m_new[None])
            l_new = l_new + jnp.sum(pr, axis=0)
            acc = acc + jnp.sum(pr * vc_refs[j][0], axis=0)
        l_ref[c] = l_new
        acc_ref[c] = acc
        m_ref[c] = m_new

    @pl.when(p == pl.num_programs(1) - 1)
    def _():
        lam = _lambda_value(lam_ref, lam_init)
        o = acc_ref[0] / l_ref[0] - lam * (acc_ref[1] / l_ref[1])
        o = o * lax.rsqrt(jnp.mean(o * o, axis=-1, keepdims=True) + RMS_EPS) * w_ref[...] * (1.0 - lam_init)
        o_ref[0] = o.astype(o_ref.dtype)


def attn_decode(q, k_new, v_new, cache_k, cache_v, page_table, lam_vec, subln_w, *, lam_init):
    b, heads, _ = q.shape
    n_pages = page_table.shape[1]
    page = cache_k.shape[1]
    pg = math.gcd(n_pages, DECODE_PAGES_PER_STEP)
    tok = pl.BlockSpec((1, heads, LANES), lambda i, p, pt: (i, 0, 0))

    def cache(j):
        return pl.BlockSpec((1, page, heads, LANES), lambda i, p, pt: (pt[i * n_pages + p * pg + j], 0, 0, 0))

    grid_spec = pltpu.PrefetchScalarGridSpec(
        num_scalar_prefetch=1,
        grid=(b, n_pages // pg),
        in_specs=[pl.BlockSpec(lam_vec.shape, lambda i, p, pt: (0, 0)), tok, tok, tok]
        + [cache(j) for j in range(pg)] * 2
        + [pl.BlockSpec((1, LANES), lambda i, p, pt: (0, 0))],
        out_specs=tok,
        scratch_shapes=[pltpu.VMEM((2, heads, 1), F32), pltpu.VMEM((2, heads, 1), F32),
                        pltpu.VMEM((2, heads, LANES), F32)],
    )
    return pl.pallas_call(
        functools.partial(_attn_decode_kernel, lam_init=lam_init, pg=pg),
        out_shape=jax.ShapeDtypeStruct(q.shape, BF16),
        grid_spec=grid_spec,
        compiler_params=_params("parallel", "arbitrary"),
        name="attn_decode",
    )(page_table.reshape(-1), lam_vec, q, k_new, v_new, *([cache_k] * pg), *([cache_v] * pg),
      subln_w.reshape(1, LANES))


def _router_kernel(x_ref, sc_ref, sh_ref, wr_ref, br_ref, idx_ref, gate_ref, pos_ref, cnt_ref,
                   tri_ref, carry_ref, *, tm):
    i = pl.program_id(0)

    @pl.when(i == 0)
    def _():
        r = lax.broadcasted_iota(I32, (tm, tm), 0)
        c = lax.broadcasted_iota(I32, (tm, tm), 1)
        tri_ref[...] = jnp.where(c < r, 1.0, 0.0).astype(BF16)
        carry_ref[...] = jnp.zeros_like(carry_ref)

    h = x_ref[...] * (1.0 + sc_ref[0]) + sh_ref[0]
    h_hi = h.astype(BF16)
    h_lo = (h - h_hi.astype(F32)).astype(BF16)
    w = wr_ref[...]
    w_hi = w.astype(BF16)
    w_lo = (w - w_hi.astype(F32)).astype(BF16)
    logits = _dot(h_hi, w_hi) + _dot(h_lo, w_hi) + _dot(h_hi, w_lo) + br_ref[...]
    lane = lax.broadcasted_iota(I32, (1, LANES), 1)
    work = jnp.where(lane < N_EXPERTS, logits, NEG_BIG)
    tops, idxs = [], []
    member = jnp.zeros((tm, LANES), F32)
    for _ in range(TOP_K):
        mx = jnp.max(work, axis=1, keepdims=True)
        ix = jnp.min(jnp.where(work == mx, lane, LANES), axis=1, keepdims=True)
        hit = lane == ix
        member = jnp.where(hit, 1.0, member)
        work = jnp.where(hit, NEG_BIG, work)
        tops.append(mx)
        idxs.append(ix)
    ex = [jnp.exp(t - tops[0]) for t in tops]
    denom = ex[0] + ex[1] + ex[2] + ex[3]
    rank = _dot(tri_ref[...], member.astype(BF16)) + carry_ref[...]
    idx_out = jnp.zeros((tm, LANES), I32)
    gate_out = jnp.zeros((tm, LANES), F32)
    pos_out = jnp.zeros((tm, LANES), I32)
    for k in range(TOP_K):
        pk = jnp.sum(jnp.where(lane == idxs[k], rank, 0.0), axis=1, keepdims=True)
        idx_out = jnp.where(lane == k, idxs[k], idx_out)
        gate_out = jnp.where(lane == k, ex[k] / denom, gate_out)
        pos_out = jnp.where(lane == k, pk.astype(I32), pos_out)
    idx_ref[...] = idx_out
    gate_ref[...] = gate_out
    pos_ref[...] = pos_out
    carry_ref[...] = carry_ref[...] + jnp.sum(member, axis=0, keepdims=True)
    cnt_ref[...] = carry_ref[...].astype(I32)


def router(x, sc, sh, w_r, b_r, *, rows_per_group, tm):
    n, d = x.shape
    tiles_per_group = rows_per_group // tm
    r = sc.shape[1]
    w_pad = jnp.zeros((d, LANES), F32).at[:, :N_EXPERTS].set(w_r)
    b_pad = jnp.zeros((1, LANES), F32).at[0, :N_EXPERTS].set(b_r)
    mod = pl.BlockSpec((1, r, d), lambda i: (i // tiles_per_group, 0, 0))
    wide = pl.BlockSpec((tm, LANES), lambda i: (i, 0))
    return pl.pallas_call(
        functools.partial(_router_kernel, tm=tm),
        out_shape=(jax.ShapeDtypeStruct((n, LANES), I32), jax.ShapeDtypeStruct((n, LANES), F32),
                   jax.ShapeDtypeStruct((n, LANES), I32), jax.ShapeDtypeStruct((1, LANES), I32)),
        grid=(n // tm,),
        in_specs=[pl.BlockSpec((tm, d), lambda i: (i, 0)), mod, mod,
                  pl.BlockSpec((d, LANES), lambda i: (0, 0)), pl.BlockSpec((1, LANES), lambda i: (0, 0))],
        out_specs=(wide, wide, wide, pl.BlockSpec((1, LANES), lambda i: (0, 0))),
        scratch_shapes=[pltpu.VMEM((tm, tm), BF16), pltpu.VMEM((1, LANES), F32)],
        compiler_params=_params("arbitrary"),
        name="moe_router",
    )(x, sc, sh, w_pad, b_pad)


def _dispatch_kernel(gap_lo_ref, gap_hi_ref, x_ref, sc_ref, sh_ref, dest_ref, xs_ref,
                     h_ref, zero_ref, idx_ref, sem_ref, isem_ref, *, tm, n_gap_rows):
    i = pl.program_id(0)
    n_idx = tm * TOP_K
    idx_copy = pltpu.make_async_copy(dest_ref.at[0, 0], idx_ref, isem_ref)
    idx_copy.start()
    h = x_ref[...] * (1.0 + sc_ref[0]) + sh_ref[0]
    for g in range(h.shape[1] // LANES):
        h_ref[:, g, :] = h[:, g * LANES:(g + 1) * LANES]
    idx_copy.wait()

    def row_copy(src_row, dst_row):
        return pltpu.make_async_copy(src_row, xs_ref.at[pl.ds(dst_row, 1)], sem_ref)

    def issue(r, carry):
        for k in range(TOP_K):
            row_copy(h_ref.at[pl.ds(r, 1)], idx_ref[r * TOP_K + k]).start()
        return carry

    lax.fori_loop(0, tm, issue, 0, unroll=DMA_UNROLL)

    def drain(r, carry):
        row_copy(h_ref.at[pl.ds(0, 1)], 0).wait()
        return carry

    lax.fori_loop(0, n_idx, drain, 0, unroll=DMA_UNROLL)

    @pl.when(i == 0)
    def _():
        zero_ref[...] = jnp.zeros_like(zero_ref)

        def per_expert(e, carry):
            def fill(rw, c2):
                row_copy(zero_ref.at[pl.ds(0, 1)], rw).start()
                return c2
            return lax.fori_loop(gap_lo_ref[e], gap_hi_ref[e], fill, carry)

        lax.fori_loop(0, N_EXPERTS, per_expert, 0)

        def drain_zero(r, carry):
            row_copy(zero_ref.at[pl.ds(0, 1)], 0).wait()
            return carry

        lax.fori_loop(0, n_gap_rows, drain_zero, 0)


def dispatch(x, sc, sh, dest, gap_lo, gap_hi, *, rows_per_group, tm, n_slots_padded):
    n, d = x.shape
    tiles_per_group = rows_per_group // tm
    r = sc.shape[1]
    n_idx = tm * TOP_K
    mod = pl.BlockSpec((1, r, d), lambda i, lo, hi: (i // tiles_per_group, 0, 0))
    grid_spec = pltpu.PrefetchScalarGridSpec(
        num_scalar_prefetch=2,
        grid=(n // tm,),
        in_specs=[pl.BlockSpec((tm, d), lambda i, lo, hi: (i, 0)), mod, mod,
                  pl.BlockSpec((1, 1, n_idx), lambda i, lo, hi: (i, 0, 0))],
        out_specs=pl.BlockSpec(memory_space=pl.ANY),
        scratch_shapes=[pltpu.VMEM((tm, d // LANES, LANES), F32), pltpu.VMEM((1, d // LANES, LANES), F32),
                        pltpu.SMEM((n_idx,), I32), pltpu.SemaphoreType.DMA, pltpu.SemaphoreType.DMA],
    )
    return pl.pallas_call(
        functools.partial(_dispatch_kernel, tm=tm, n_gap_rows=n_slots_padded - n * TOP_K),
        out_shape=jax.ShapeDtypeStruct((n_slots_padded, d // LANES, LANES), F32),
        grid_spec=grid_spec,
        compiler_params=_params("arbitrary"),
        name="moe_dispatch",
    )(gap_lo, gap_hi, x, sc, sh, dest.reshape(n // tm, 1, n_idx))


def _prep_expert_kernel(wgu_ref, wd_ref, wgu_out, wd_out):
    half = GU_GROUP // 2
    r = lax.broadcasted_iota(I32, (GU_GROUP, GU_GROUP), 0)
    c = lax.broadcasted_iota(I32, (GU_GROUP, GU_GROUP), 1)
    perm = jnp.where(r == jnp.where(c < half, 2 * c, 2 * (c - half) + 1), 1.0, 0.0).astype(BF16)
    for g in range(wgu_ref.shape[2] // GU_GROUP):
        cols = slice(g * GU_GROUP, (g + 1) * GU_GROUP)
        wgu_out[0, :, cols] = _dot(wgu_ref[0, :, cols].astype(BF16), perm).astype(BF16)
    wd_out[0] = wd_ref[0].astype(BF16)


def prep_expert_weights(w_gu, w_down):
    ne, d, de2 = w_gu.shape
    gu_spec = pl.BlockSpec((1, d, de2), lambda e: (e, 0, 0))
    dn_spec = pl.BlockSpec((1, de2 // 2, d), lambda e: (e, 0, 0))
    return pl.pallas_call(
        _prep_expert_kernel,
        out_shape=(jax.ShapeDtypeStruct(w_gu.shape, BF16), jax.ShapeDtypeStruct(w_down.shape, BF16)),
        grid=(ne,),
        in_specs=[gu_spec, dn_spec],
        out_specs=(gu_spec, dn_spec),
        compiler_params=_params("parallel"),
        name="moe_prep_weights",
    )(w_gu, w_down)


def _expert_ffn_kernel(be_ref, xs_ref, wgu_ref, bgu_ref, wd_ref, bd_ref, ys_ref):
    n_lane_groups = xs_ref.shape[1]
    x = jnp.concatenate([xs_ref[:, g, :] for g in range(n_lane_groups)], axis=1).astype(BF16)
    gu = _dot(x, wgu_ref[0]) + bgu_ref[0]
    half = GU_GROUP // 2
    acts = []
    for g in range(gu.shape[1] // GU_GROUP):
        gate = jnp.minimum(gu[:, g * GU_GROUP:g * GU_GROUP + half], SWIGLU_LIMIT)
        up = jnp.clip(gu[:, g * GU_GROUP + half:(g + 1) * GU_GROUP], -SWIGLU_LIMIT, SWIGLU_LIMIT)
        acts.append((up + 1.0) * (gate * _sigmoid(gate * SWIGLU_ALPHA)))
    act = jnp.concatenate(acts, axis=1).astype(BF16)
    y = _dot(act, wd_ref[0]) + bd_ref[0]
    for g in range(n_lane_groups):
        ys_ref[:, g, :] = y[:, g * LANES:(g + 1) * LANES]


def expert_ffn(xs, block_expert, wgu, bgu, wd, bd, *, blk):
    ns, lg, _ = xs.shape
    d, de2 = wgu.shape[1], wgu.shape[2]
    rows = pl.BlockSpec((blk, lg, LANES), lambda i, be: (i, 0, 0))
    wspec = lambda a, b: pl.BlockSpec((1, a, b), lambda i, be: (be[i], 0, 0))
    grid_spec = pltpu.PrefetchScalarGridSpec(
        num_scalar_prefetch=1,
        grid=(ns // blk,),
        in_specs=[rows, wspec(d, de2), wspec(1, de2), wspec(de2 // 2, d), wspec(1, d)],
        out_specs=rows,
    )
    return pl.pallas_call(
        _expert_ffn_kernel,
        out_shape=jax.ShapeDtypeStruct(xs.shape, F32),
        grid_spec=grid_spec,
        compiler_params=_params("arbitrary"),
        name="moe_expert_ffn",
    )(block_expert, xs, wgu, bgu, wd, bd)


def _combine_ln_kernel(x_ref, gm_ref, gate_ref, dest_ref, ys_ref, lg_ref, lb_ref, o_ref,
                       buf_ref, idx_ref, sem_ref, isem_ref, *, tm, alpha):
    n_idx = tm * TOP_K
    idx_copy = pltpu.make_async_copy(dest_ref.at[0, 0], idx_ref, isem_ref)
    idx_copy.start()
    idx_copy.wait()

    def row_copy(src_row, k, r):
        return pltpu.make_async_copy(ys_ref.at[pl.ds(src_row, 1)], buf_ref.at[k, pl.ds(r, 1)], sem_ref)

    def issue(r, carry):
        for k in range(TOP_K):
            row_copy(idx_ref[r * TOP_K + k], k, r).start()
        return carry

    lax.fori_loop(0, tm, issue, 0, unroll=DMA_UNROLL)

    def drain(r, carry):
        row_copy(0, 0, 0).wait()
        return carry

    lax.fori_loop(0, n_idx, drain, 0, unroll=DMA_UNROLL)
    gates = gate_ref[...]
    cols = []
    for g in range(buf_ref.shape[2]):
        yg = gates[:, 0:1] * buf_ref[0, :, g, :]
        for k in range(1, TOP_K):
            yg = yg + gates[:, k:k + 1] * buf_ref[k, :, g, :]
        cols.append(yg)
    y = jnp.concatenate(cols, axis=1)
    o_ref[...] = _layer_norm(alpha * x_ref[...] + gm_ref[0] * y, lg_ref[...], lb_ref[...])


def combine_ln(x, gm, gates, dest, ys, ln_g, ln_b, *, rows_per_group, tm, alpha):
    n, d = x.shape
    tiles_per_group = rows_per_group // tm
    r = gm.shape[1]
    n_idx = tm * TOP_K
    row = pl.BlockSpec((tm, d), lambda i: (i, 0))
    vec = pl.BlockSpec((1, d), lambda i: (0, 0))
    return pl.pallas_call(
        functools.partial(_combine_ln_kernel, tm=tm, alpha=alpha),
        out_shape=jax.ShapeDtypeStruct((n, d), F32),
        grid=(n // tm,),
        in_specs=[row, pl.BlockSpec((1, r, d), lambda i: (i // tiles_per_group, 0, 0)),
                  pl.BlockSpec((tm, LANES), lambda i: (i, 0)),
                  pl.BlockSpec((1, 1, n_idx), lambda i: (i, 0, 0)),
                  pl.BlockSpec(memory_space=pl.ANY), vec, vec],
        out_specs=row,
        scratch_shapes=[pltpu.VMEM((TOP_K, tm, d // LANES, LANES), F32), pltpu.SMEM((n_idx,), I32),
                        pltpu.SemaphoreType.DMA, pltpu.SemaphoreType.DMA],
        compiler_params=_params("arbitrary"),
        name="moe_combine_ln",
    )(x, gm, gates, dest.reshape(n // tm, 1, n_idx), ys, ln_g.reshape(1, d), ln_b.reshape(1, d))


def _moe_layer(x, sc, sh, gm, ln_g, ln_b, w_r, b_r, wgu, bgu, wd, bd, *, rows_per_group, tm, blk, alpha):
    n, d = x.shape
    idx_w, gate_w, pos_w, cnt_w = router(x, sc, sh, w_r, b_r, rows_per_group=rows_per_group, tm=tm)
    counts = cnt_w[0, :N_EXPERTS]
    top_idx, pos = idx_w[:, :TOP_K], pos_w[:, :TOP_K]
    n_slots = n * TOP_K
    n_blocks = -(-(n_slots + N_EXPERTS * (blk - 1)) // blk)
    padded = (counts + blk - 1) // blk * blk
    pad_end = jnp.cumsum(padded)
    pad_start = pad_end - padded
    onehot = top_idx[..., None] == jnp.arange(N_EXPERTS, dtype=I32)
    dest = (jnp.sum(jnp.where(onehot, pad_start, 0), axis=-1) + pos).astype(I32)
    block_start = jnp.arange(n_blocks, dtype=I32) * blk
    block_expert = jnp.minimum(
        jnp.sum((pad_end[None, :] <= block_start[:, None]).astype(I32), axis=1), N_EXPERTS - 1).astype(I32)
    gap_lo = (pad_start + counts).astype(I32)
    gap_hi = jnp.concatenate([pad_start[1:], jnp.full((1,), n_blocks * blk, I32)]).astype(I32)
    xs = dispatch(x, sc, sh, dest, gap_lo, gap_hi, rows_per_group=rows_per_group, tm=tm,
                  n_slots_padded=n_blocks * blk)
    ys = expert_ffn(xs, block_expert, wgu, bgu, wd, bd, blk=blk)
    return combine_ln(x, gm, gate_w, dest, ys, ln_g, ln_b, rows_per_group=rows_per_group, tm=tm, alpha=alpha)


def _rope_tables(pos):
    inv_freq = ROPE_THETA ** (-jnp.arange(ROPE_HALF, dtype=F32) / ROPE_HALF)
    ang = pos.astype(F32)[:, None] * inv_freq[None, :]
    cos, sin = jnp.cos(ang), jnp.sin(ang)
    reps = LANES // DIFF_HEAD_DIM
    return (jnp.concatenate([cos, cos] * reps, axis=1), jnp.concatenate([-sin, sin] * reps, axis=1))


def _trunk(x, mods, kv_mods, past_len, s0, cache, wts, *, batch, seq, tm, blk):
    n, d = x.shape
    depth = wts["ada_w"].shape[0]
    n_a = wts["hgrn_w_in"].shape[0]
    alpha = (2 * depth) ** 0.25
    rows_per_group = seq if seq >= tm else n
    kw = dict(rows_per_group=rows_per_group, tm=tm)
    decode = seq == 1
    pos = past_len + (jnp.zeros((n,), I32) if decode else jnp.arange(seq, dtype=I32))
    rope = _rope_tables(pos)
    heads = d // LANES
    states = []
    k_new = v_new = k16 = v16 = None
    for l in range(depth):
        sh_m, sc_m, g_m, sh_f, sc_f, g_f = mods[l]
        if l < n_a:
            proj, = modmm(x, sc_m, sh_m, wts["hgrn_w_in"][l], out_dtypes=(F32,), **kw)
            lbs = [a[l][None] for a in wts["lb"]]
            nw = wts["hgrn_norm_w"][l].reshape(1, LANES)
            if decode:
                o, s_l = hgrn_step(proj, *lbs, nw, s0[l])
            else:
                o, s_l = hgrn_chunked(proj, *lbs, nw, s0[l], batch=batch, seq=seq, tb=min(seq, 512))
            states.append(s_l)
            w_out = wts["hgrn_w_out"][l]
        else:
            j = l - n_a
            lam_init = 0.8 - 0.6 * math.exp(-0.3 * l)
            if l == n_a:
                kv_shift, kv_scale = kv_mods
                half = wts["kv_w"].shape[1] // 2
                k_new, k16 = modmm(x, kv_scale, kv_shift, wts["kv_w"][:, :half], out_dtypes=(F32, BF16),
                                   rope=rope, **kw)
                v_new, v16 = modmm(x, kv_scale, kv_shift, wts["kv_w"][:, half:], out_dtypes=(F32, BF16), **kw)
            q_scale = DIFF_HEAD_DIM ** -0.5
            if decode:
                q, = modmm(x, sc_m, sh_m, wts["diff_w_q"][j], out_dtypes=(F32,), rope=rope, out_scale=q_scale, **kw)
                o = attn_decode(q.reshape(n, heads, LANES), k_new.reshape(n, heads, LANES),
                                v_new.reshape(n, heads, LANES), cache[0], cache[1], cache[2],
                                wts["diff_lambda"][j], wts["diff_subln_w"][j], lam_init=lam_init)
                o = o.reshape(n, d)
            else:
                q, = modmm(x, sc_m, sh_m, wts["diff_w_q"][j], out_dtypes=(BF16,), rope=rope, out_scale=q_scale, **kw)
                o = attn_prefill(q, k16, v16, wts["diff_lambda"][j], wts["diff_subln_w"][j],
                                 batch=batch, seq=seq, tq=min(seq, 512), lam_init=lam_init)
            w_out = wts["diff_w_out"][j]
        x = proj_ln(o, w_out, x, g_m, wts["ln_g"][l, 0], wts["ln_b"][l, 0], alpha=alpha, **kw)
        x = _moe_layer(x, sc_f, sh_f, g_f, wts["ln_g"][l, 1], wts["ln_b"][l, 1],
                       wts["moe_w_router"][l], wts["moe_b_router"][l],
                       wts["moe_w_gu"][l], wts["moe_b_gu"][l], wts["moe_w_down"][l], wts["moe_b_down"][l],
                       alpha=alpha, blk=blk, **kw)
    return x, k_new, v_new, jnp.stack(states)


def kernel(x_prompt, x_sample, cache_k, cache_v, state_hgrn, page_table, c_prompt, c_sample, ada_w, ada_b, ln_g, ln_b, hgrn_w_in, hgrn_lb_logits, hgrn_norm_w, hgrn_w_out, kv_ada_w, kv_ada_b, kv_w, diff_w_q, diff_lambda, diff_subln_w, diff_w_out, moe_w_router, moe_b_router, moe_w_gu, moe_b_gu, moe_w_down, moe_b_down):
    b_p, t_p, d = x_prompt.shape
    b_d, t_d, _ = x_sample.shape
    depth = ada_w.shape[0]
    n_a = hgrn_w_in.shape[0]
    heads = d // LANES
    de = moe_w_down.shape[2]

    m_rows = -(-(b_d + b_p) // 8) * 8
    c_all = jnp.zeros((m_rows, d), F32).at[:b_d].set(c_sample).at[b_d:b_d + b_p].set(c_prompt)
    mod_all = adaln(c_all, ada_w, ada_b)
    kv_mod_all = adaln(c_all, kv_ada_w[None], kv_ada_b[None])[0]

    def split_mods(m, n_vec, lo, hi, per_row):
        parts = [m[lo:hi, v * d:(v + 1) * d] for v in range(n_vec)]
        return [p[None] if per_row else p[:, None, :] for p in parts]

    mods_p = [split_mods(mod_all[l], 6, b_d, b_d + b_p, False) for l in range(depth)]
    mods_s = [split_mods(mod_all[l], 6, 0, b_d, True) for l in range(depth)]
    kv_mods_p = split_mods(kv_mod_all, 2, b_d, b_d + b_p, False)
    kv_mods_s = split_mods(kv_mod_all, 2, 0, b_d, True)

    loglb, log1m, onem = hgrn_lower_bounds(hgrn_lb_logits)

    n_exp = moe_w_gu.shape[1]
    w_gu16, w_down16 = prep_expert_weights(moe_w_gu.reshape(depth * n_exp, d, 2 * de),
                                           moe_w_down.reshape(depth * n_exp, de, d))
    half = GU_GROUP // 2
    b_gu = moe_b_gu.reshape(depth, n_exp, 2 * de // GU_GROUP, half, 2)
    b_gu = jnp.swapaxes(b_gu, -1, -2).reshape(depth, n_exp, 1, 2 * de)

    wts = dict(
        ada_w=ada_w, ln_g=ln_g, ln_b=ln_b,
        hgrn_w_in=hgrn_w_in.astype(BF16), lb=(loglb, log1m, onem), hgrn_norm_w=hgrn_norm_w,
        hgrn_w_out=hgrn_w_out.astype(BF16), kv_w=kv_w.astype(BF16), diff_w_q=diff_w_q.astype(BF16),
        diff_lambda=diff_lambda, diff_subln_w=diff_subln_w, diff_w_out=diff_w_out.astype(BF16),
        moe_w_router=moe_w_router, moe_b_router=moe_b_router,
        moe_w_gu=w_gu16.reshape(depth, n_exp, d, 2 * de), moe_b_gu=b_gu,
        moe_w_down=w_down16.reshape(depth, n_exp, de, d), moe_b_down=moe_b_down.reshape(depth, n_exp, 1, d),
    )

    s0_prompt = jnp.zeros((n_a, b_p, heads, LANES, LANES), F32)
    y_p, k_p, v_p, st_p = _trunk(x_prompt.reshape(b_p * t_p, d), mods_p, kv_mods_p, 0, s0_prompt, None, wts,
                                 batch=b_p, seq=t_p, tm=min(512, t_p), blk=256)
    past_len = page_table.shape[1] * cache_k.shape[1]
    y_s, k_s, v_s, st_s = _trunk(x_sample.reshape(b_d * t_d, d), mods_s, kv_mods_s, past_len, state_hgrn,
                                 (cache_k, cache_v, page_table), wts, batch=b_d, seq=t_d, tm=b_d, blk=16)
    dv = d // heads
    return (y_p.reshape(b_p, t_p, d), y_s.reshape(b_d, t_d, d),
            k_p.reshape(b_p, t_p, heads, dv), v_p.reshape(b_p, t_p, heads, dv), st_p,
            k_s.reshape(b_d, t_d, heads, dv), v_s.reshape(b_d, t_d, heads, dv), st_s)
```

```python
import functools
import math

import jax
import jax.numpy as jnp
from jax import lax
from jax.experimental import pallas as pl
from jax.experimental.pallas import tpu as pltpu

F32 = jnp.float32
BF16 = jnp.bfloat16
I32 = jnp.int32

LANES = 128
HGRN_CHUNK = 64
HGRN_SUB = 16
LB_FLOOR = 1e-30
DIFF_HEAD_DIM = 64
ROPE_THETA = 10000.0
ROPE_HALF = DIFF_HEAD_DIM // 2
NEG_BIG = -1e30
N_EXPERTS = 32
TOP_K = 4
SWIGLU_LIMIT = 7.0
SWIGLU_ALPHA = 1.702
LN_EPS = 1e-5
RMS_EPS = 1e-5
VMEM_LIMIT = 56 * 1024 * 1024
DECODE_PAGES_PER_STEP = 8
GU_GROUP = 256
DMA_UNROLL = 8


def _params(*sem):
    return pltpu.CompilerParams(dimension_semantics=sem, vmem_limit_bytes=VMEM_LIMIT)


def _sigmoid(x):
    return 1.0 / (1.0 + jnp.exp(-x))


def _silu(x):
    return x * _sigmoid(x)


def _log_sigmoid(x):
    return jnp.minimum(x, 0.0) - jnp.log1p(jnp.exp(-jnp.abs(x)))


def _layer_norm(z, g, b):
    mu = jnp.mean(z, axis=-1, keepdims=True)
    zc = z - mu
    var = jnp.mean(zc * zc, axis=-1, keepdims=True)
    return zc * lax.rsqrt(var + LN_EPS) * g + b


def _dot(a, b):
    return jnp.dot(a, b, preferred_element_type=F32)


def _dot_nt(a, b):
    return lax.dot_general(a, b, (((1,), (1,)), ((), ())), preferred_element_type=F32)


def _dot_tn(a, b):
    return lax.dot_general(a, b, (((0,), (0,)), ((), ())), preferred_element_type=F32)


def _split3(x):
    hi = x.astype(BF16)
    r1 = x - hi.astype(F32)
    mid = r1.astype(BF16)
    lo = (r1 - mid.astype(F32)).astype(BF16)
    return hi, mid, lo


def _adaln_kernel(c_ref, w_ref, b_ref, o_ref):
    s = _silu(c_ref[...]).astype(BF16)
    o_ref[0] = _dot(s, w_ref[0].astype(BF16)) + b_ref[0]


def adaln(c, w, b, tn=1024):
    m, d = c.shape
    nl, _, no = w.shape
    return pl.pallas_call(
        _adaln_kernel,
        out_shape=jax.ShapeDtypeStruct((nl, m, no), F32),
        grid=(nl, no // tn),
        in_specs=[
            pl.BlockSpec((m, d), lambda l, j: (0, 0)),
            pl.BlockSpec((1, d, tn), lambda l, j: (l, 0, j)),
            pl.BlockSpec((1, 1, tn), lambda l, j: (l, 0, j)),
        ],
        out_specs=pl.BlockSpec((1, m, tn), lambda l, j: (l, 0, j)),
        compiler_params=_params("parallel", "parallel"),
        name="adaln",
    )(c, w, b.reshape(nl, 1, no))


def _lb_kernel(logit_ref, loglb_ref, log1m_ref, onem_ref):
    x = logit_ref[...]
    nl = x.shape[0]
    e = jnp.exp(x - jnp.max(x, axis=0, keepdims=True))
    sm = e / jnp.sum(e, axis=0, keepdims=True)
    acc = jnp.zeros_like(sm[0:1])
    for l in range(nl):
        lb = jnp.clip(acc, 0.0, 1.0 - 1e-6)
        loglb_ref[l] = jnp.log(jnp.maximum(lb, LB_FLOOR))
        log1m_ref[l] = jnp.log1p(-lb)
        onem_ref[l] = 1.0 - lb
        acc = acc + sm[l:l + 1]


def hgrn_lower_bounds(lb_logits):
    nl, f = lb_logits.shape
    shp = jax.ShapeDtypeStruct((nl, 1, f), F32)
    return pl.pallas_call(_lb_kernel, out_shape=(shp, shp, shp), name="hgrn_lb")(lb_logits)


def _rotary_tile(y, cos, sin_signed):
    lane = lax.broadcasted_iota(I32, (1, LANES), 1)
    first_half = (lane % DIFF_HEAD_DIM) < ROPE_HALF
    partner = jnp.where(first_half, pltpu.roll(y, LANES - ROPE_HALF, 1), pltpu.roll(y, ROPE_HALF, 1))
    return y * cos + partner * sin_signed


def _modmm_kernel(*refs, rotary, out_scale, n_out):
    if rotary:
        x_ref, sc_ref, sh_ref, w_ref, cos_ref, sin_ref = refs[:6]
        rest = refs[6:]
    else:
        x_ref, sc_ref, sh_ref, w_ref = refs[:4]
        rest = refs[4:]
    out_refs, h_ref = rest[:n_out], rest[n_out]

    @pl.when(pl.program_id(1) == 0)
    def _():
        h_ref[...] = (x_ref[...] * (1.0 + sc_ref[0]) + sh_ref[0]).astype(BF16)

    y = _dot(h_ref[...], w_ref[...])
    if rotary:
        cos, sin = cos_ref[...], sin_ref[...]
        y = jnp.concatenate(
            [_rotary_tile(y[:, g * LANES:(g + 1) * LANES], cos, sin) for g in range(y.shape[1] // LANES)], axis=1)
    if out_scale != 1.0:
        y = y * out_scale
    for o_ref in out_refs:
        o_ref[...] = y.astype(o_ref.dtype)


def modmm(x, sc, sh, w, *, rows_per_group, tm, out_dtypes, rope=None, out_scale=1.0, tn=1024):
    n, d = x.shape
    no = w.shape[1]
    tiles_per_group = rows_per_group // tm
    r = sc.shape[1]
    in_specs = [
        pl.BlockSpec((tm, d), lambda i, j: (i, 0)),
        pl.BlockSpec((1, r, d), lambda i, j: (i // tiles_per_group, 0, 0)),
        pl.BlockSpec((1, r, d), lambda i, j: (i // tiles_per_group, 0, 0)),
        pl.BlockSpec((d, tn), lambda i, j: (0, j)),
    ]
    args = [x, sc, sh, w]
    if rope is not None:
        pos_tiles = rope[0].shape[0] // tm
        in_specs += [pl.BlockSpec((tm, LANES), lambda i, j: (i % pos_tiles, 0))] * 2
        args += list(rope)
    outs = tuple(jax.ShapeDtypeStruct((n, no), dt) for dt in out_dtypes)
    res = pl.pallas_call(
        functools.partial(_modmm_kernel, rotary=rope is not None, out_scale=out_scale, n_out=len(outs)),
        out_shape=outs,
        grid=(n // tm, no // tn),
        in_specs=in_specs,
        out_specs=tuple(pl.BlockSpec((tm, tn), lambda i, j: (i, j)) for _ in outs),
        scratch_shapes=[pltpu.VMEM((tm, d), BF16)],
        compiler_params=_params("parallel", "arbitrary"),
        name="modmm",
    )(*args)
    return res


def _hgrn_head_chunk(q_raw, f, v, g_raw, loglb, log1m, onem, nw, st, tri, sub_row):
    c, sub = HGRN_CHUNK, HGRN_SUB
    q = _silu(q_raw)
    x2 = log1m + _log_sigmoid(f)
    mx = jnp.maximum(loglb, x2)
    logf = mx + jnp.log1p(jnp.exp(jnp.minimum(loglb, x2) - mx))
    kk = onem * _sigmoid(-f)
    hi, mid, lo = _split3(logf)
    b = _dot(tri, hi) + _dot(tri, mid) + _dot(tri, lo)
    v16 = v.astype(BF16)
    o_inter = _dot_nt((q * jnp.exp(b)).astype(BF16), st.astype(BF16))
    o_parts = []
    for i in range(c // sub):
        sl = slice(i * sub, (i + 1) * sub)
        qi, bi, ki, vi = q[sl], b[sl], kk[sl], v[sl]
        oi = o_inter[sl]
        for s in range(sub):
            w = qi * ki[s:s + 1] * jnp.exp(jnp.minimum(bi - bi[s:s + 1], 0.0))
            a = jnp.sum(w, axis=1, keepdims=True)
            a = jnp.where(sub_row >= s, a, 0.0)
            oi = oi + a * vi[s:s + 1]
        if i > 0:
            r = b[i * sub - 1:i * sub]
            qs = (qi * jnp.exp(bi - r)).astype(BF16)
            ks = (kk[:i * sub] * jnp.exp(r - b[:i * sub])).astype(BF16)
            a = _dot_nt(qs, ks)
            oi = oi + _dot(a.astype(BF16), v16[:i * sub])
        o_parts.append(oi)
    o = jnp.concatenate(o_parts, axis=0)
    b_last = b[c - 1:c]
    kd = (kk * jnp.exp(b_last - b)).astype(BF16)
    st_new = st * jnp.exp(b_last) + _dot_tn(v16, kd)
    o = o * lax.rsqrt(jnp.mean(o * o, axis=-1, keepdims=True) + RMS_EPS) * nw * _silu(g_raw)
    return o, st_new


def _hgrn_chunk_kernel(q_ref, f_ref, v_ref, g_ref, loglb_ref, log1m_ref, onem_ref, nw_ref, s0_ref,
                       o_ref, sfin_ref, st_ref, *, n_chunks, hg):
    t = pl.program_id(2)
    c = HGRN_CHUNK

    @pl.when(t == 0)
    def _():
        for hh in range(hg):
            st_ref[hh] = s0_ref[0, hh].T

    loglb, log1m, onem, nw = loglb_ref[0], log1m_ref[0], onem_ref[0], nw_ref[...]
    row = lax.broadcasted_iota(I32, (c, c), 0)
    col = lax.broadcasted_iota(I32, (c, c), 1)
    tri = jnp.where(col <= row, 1.0, 0.0).astype(BF16)
    sub_row = lax.broadcasted_iota(I32, (HGRN_SUB, 1), 0)

    def chunk(ci, carry):
        rows = pl.ds(pl.multiple_of(ci * c, c), c)
        outs = []
        for hh in range(hg):
            ln = slice(hh * LANES, (hh + 1) * LANES)
            o, st_new = _hgrn_head_chunk(q_ref[rows, ln], f_ref[rows, ln], v_ref[rows, ln], g_ref[rows, ln],
                                         loglb[:, ln], log1m[:, ln], onem[:, ln], nw, st_ref[hh], tri, sub_row)
            st_ref[hh] = st_new
            outs.append(o)
        o_ref[rows, :] = jnp.concatenate(outs, axis=1).astype(o_ref.dtype)
        return carry

    lax.fori_loop(0, n_chunks, chunk, 0)

    @pl.when(t == pl.num_programs(2) - 1)
    def _():
        for hh in range(hg):
            sfin_ref[0, hh] = st_ref[hh].T


def hgrn_chunked(proj, loglb, log1m, onem, norm_w, s0, *, batch, seq, tb, hg=8):
    n = proj.shape[0]
    heads = proj.shape[1] // (4 * LANES)
    tpb = seq // tb
    ngrp = heads // hg
    w = hg * LANES
    row_blk = lambda off: pl.BlockSpec((tb, w), lambda b, h, t: (b * tpb + t, off * ngrp + h))
    vec = pl.BlockSpec((1, 1, w), lambda b, h, t: (0, 0, h))
    st_spec = pl.BlockSpec((1, hg, LANES, LANES), lambda b, h, t: (b, h, 0, 0))
    return pl.pallas_call(
        functools.partial(_hgrn_chunk_kernel, n_chunks=tb // HGRN_CHUNK, hg=hg),
        out_shape=(jax.ShapeDtypeStruct((n, heads * LANES), BF16),
                   jax.ShapeDtypeStruct(s0.shape, F32)),
        grid=(batch, ngrp, tpb),
        in_specs=[row_blk(0), row_blk(1), row_blk(2), row_blk(3), vec, vec, vec,
                  pl.BlockSpec((1, LANES), lambda b, h, t: (0, 0)), st_spec],
        out_specs=(pl.BlockSpec((tb, w), lambda b, h, t: (b * tpb + t, h)), st_spec),
        scratch_shapes=[pltpu.VMEM((hg, LANES, LANES), F32)],
        compiler_params=_params("parallel", "parallel", "arbitrary"),
        name="hgrn_chunked",
    )(proj, proj, proj, proj, loglb, log1m, onem, norm_w, s0)


def _hgrn_step_kernel(q_ref, f_ref, v_ref, g_ref, loglb_ref, log1m_ref, onem_ref, nw_ref, s_ref,
                      o_ref, snew_ref, *, bb):
    f = f_ref[...]
    q = _silu(q_ref[...])
    v = v_ref[...]
    x1 = loglb_ref[0]
    x2 = log1m_ref[0] + _log_sigmoid(f)
    mx = jnp.maximum(x1, x2)
    decay = jnp.exp(mx + jnp.log1p(jnp.exp(jnp.minimum(x1, x2) - mx)))
    kk = onem_ref[0] * _sigmoid(-f)
    q_t, d_t, k_t = q.T, decay.T, kk.T
    rows = []
    for e in range(bb):
        s_new = s_ref[e, 0] * d_t[:, e:e + 1] + k_t[:, e:e + 1] * v[e:e + 1, :]
        snew_ref[e, 0] = s_new
        rows.append(jnp.sum(q_t[:, e:e + 1] * s_new, axis=0, keepdims=True))
    o = jnp.concatenate(rows, axis=0)
    o = o * lax.rsqrt(jnp.mean(o * o, axis=-1, keepdims=True) + RMS_EPS) * nw_ref[...] * _silu(g_ref[...])
    o_ref[...] = o.astype(o_ref.dtype)


def hgrn_step(proj, loglb, log1m, onem, norm_w, state, *, bb=16):
    n = proj.shape[0]
    heads = proj.shape[1] // (4 * LANES)
    row_blk = lambda off: pl.BlockSpec((bb, LANES), lambda i, h: (i, off * heads + h))
    vec = pl.BlockSpec((1, 1, LANES), lambda i, h: (0, 0, h))
    st_spec = pl.BlockSpec((bb, 1, LANES, LANES), lambda i, h: (i, h, 0, 0))
    return pl.pallas_call(
        functools.partial(_hgrn_step_kernel, bb=bb),
        out_shape=(jax.ShapeDtypeStruct((n, heads * LANES), BF16),
                   jax.ShapeDtypeStruct(state.shape, F32)),
        grid=(n // bb, heads),
        in_specs=[row_blk(0), row_blk(1), row_blk(2), row_blk(3), vec, vec, vec,
                  pl.BlockSpec((1, LANES), lambda i, h: (0, 0)), st_spec],
        out_specs=(pl.BlockSpec((bb, LANES), lambda i, h: (i, h)), st_spec),
        compiler_params=_params("parallel", "parallel"),
        name="hgrn_step",
    )(proj, proj, proj, proj, loglb, log1m, onem, norm_w, state)


def _proj_ln_kernel(a_ref, w_ref, x_ref, gm_ref, lg_ref, lb_ref, o_ref, *, alpha):
    y = _dot(a_ref[...], w_ref[...])
    o_ref[...] = _layer_norm(alpha * x_ref[...] + gm_ref[0] * y, lg_ref[...], lb_ref[...])


def proj_ln(a, w, x, gm, ln_g, ln_b, *, rows_per_group, tm, alpha):
    n, d = x.shape
    tiles_per_group = rows_per_group // tm
    r = gm.shape[1]
    row = pl.BlockSpec((tm, d), lambda i: (i, 0))
    vec = pl.BlockSpec((1, d), lambda i: (0, 0))
    return pl.pallas_call(
        functools.partial(_proj_ln_kernel, alpha=alpha),
        out_shape=jax.ShapeDtypeStruct((n, d), F32),
        grid=(n // tm,),
        in_specs=[row, pl.BlockSpec((d, d), lambda i: (0, 0)), row,
                  pl.BlockSpec((1, r, d), lambda i: (i // tiles_per_group, 0, 0)), vec, vec],
        out_specs=row,
        compiler_params=_params("parallel"),
        name="proj_ln",
    )(a, w, x, gm, ln_g.reshape(1, d), ln_b.reshape(1, d))


def _lambda_value(lam_ref, lam_init):
    lv = lam_ref[...]
    s1 = jnp.sum(lv[0:1] * lv[1:2], axis=1, keepdims=True)
    s2 = jnp.sum(lv[2:3] * lv[3:4], axis=1, keepdims=True)
    return jnp.exp(s1) - jnp.exp(s2) + lam_init


def _attn_prefill_kernel(lam_ref, q_ref, k_ref, v_ref, w_ref, o_ref, *, tq, lam_init):
    i = pl.program_id(2)
    q = q_ref[...]
    lane = lax.broadcasted_iota(I32, (1, LANES), 1)
    zero = jnp.zeros_like(q)
    q_c = (jnp.where(lane < DIFF_HEAD_DIM, q, zero), jnp.where(lane < DIFF_HEAD_DIM, zero, q))
    keep = lax.broadcasted_iota(I32, (tq, tq), 1) <= lax.broadcasted_iota(I32, (tq, tq), 0)
    ones = jnp.ones((tq, LANES), BF16)

    def update(s, v_aug, m, acc):
        m_new = jnp.maximum(m, jnp.max(s, axis=1, keepdims=True))
        p = jnp.exp2(s - m_new).astype(BF16)
        return m_new, jnp.exp2(m - m_new) * acc + _dot(p, v_aug)

    def block(j, carry, masked):
        rows = pl.ds(pl.multiple_of(j * tq, tq), tq)
        kj = k_ref[rows, :]
        v_aug = jnp.concatenate([v_ref[rows, :], ones], axis=1)
        s1, s2 = _dot_nt(q_c[0], kj), _dot_nt(q_c[1], kj)
        if masked:
            s1 = jnp.where(keep, s1, NEG_BIG)
            s2 = jnp.where(keep, s2, NEG_BIG)
        m1, a1, m2, a2 = carry
        return update(s1, v_aug, m1, a1) + update(s2, v_aug, m2, a2)

    m0 = jnp.full((tq, 1), NEG_BIG, F32)
    a0 = jnp.zeros((tq, 2 * LANES), F32)
    carry = lax.fori_loop(0, i, lambda j, c: block(j, c, False), (m0, a0, m0, a0))
    _, a1, _, a2 = block(i, carry, True)
    lam = _lambda_value(lam_ref, lam_init)
    o = a1[:, :LANES] / a1[:, LANES:] - lam * (a2[:, :LANES] / a2[:, LANES:])
    o = o * lax.rsqrt(jnp.mean(o * o, axis=-1, keepdims=True) + RMS_EPS) * w_ref[...] * (1.0 - lam_init)
    o_ref[...] = o.astype(o_ref.dtype)


def attn_prefill(q, k, v, lam_vec, subln_w, *, batch, seq, tq, lam_init):
    n, hd = q.shape
    heads = hd // LANES
    nq = seq // tq
    kv_spec = pl.BlockSpec((seq, LANES), lambda b, h, i: (b, h))
    q_spec = pl.BlockSpec((tq, LANES), lambda b, h, i: (b * nq + i, h))
    return pl.pallas_call(
        functools.partial(_attn_prefill_kernel, tq=tq, lam_init=lam_init),
        out_shape=jax.ShapeDtypeStruct((n, hd), BF16),
        grid=(batch, heads, nq),
        in_specs=[pl.BlockSpec(lam_vec.shape, lambda b, h, i: (0, 0)), q_spec, kv_spec, kv_spec,
                  pl.BlockSpec((1, LANES), lambda b, h, i: (0, 0))],
        out_specs=q_spec,
        compiler_params=_params("parallel", "parallel", "arbitrary"),
        name="attn_prefill",
    )(lam_vec, q, k, v, subln_w.reshape(1, LANES))


def _attn_decode_kernel(pt_ref, lam_ref, q_ref, kn_ref, vn_ref, *rest, lam_init, pg):
    kc_refs, vc_refs = rest[:pg], rest[pg:2 * pg]
    w_ref, o_ref, m_ref, l_ref, acc_ref = rest[2 * pg:]
    p = pl.program_id(1)
    q = q_ref[0]
    heads = q.shape[0]
    lane = lax.broadcasted_iota(I32, (1, LANES), 1)
    lo = lane < DIFF_HEAD_DIM
    q2 = jnp.concatenate([jnp.where(lo, q, 0.0), jnp.where(lo, 0.0, q)], axis=0)

    @pl.when(p == 0)
    def _():
        kn2 = jnp.concatenate([kn_ref[0], kn_ref[0]], axis=0)
        m_ref[...] = jnp.sum(q2 * kn2, axis=-1, keepdims=True)
        l_ref[...] = jnp.ones_like(l_ref)
        acc_ref[...] = jnp.concatenate([vn_ref[0], vn_ref[0]], axis=0)

    n_rows = kc_refs[0].shape[1] * heads
    own = (lax.broadcasted_iota(I32, (2 * heads, n_rows), 1) % heads
           == lax.broadcasted_iota(I32, (2 * heads, n_rows), 0) % heads)
    q16 = q2.astype(BF16)
    s = [jnp.where(own, _dot_nt(q16, kc_refs[j][0].reshape(n_rows, LANES).astype(BF16)), NEG_BIG)
         for j in range(pg)]
    m_old = m_ref[...]
    m_new = m_old
    for j in range(pg):
        m_new = jnp.maximum(m_new, jnp.max(s[j], axis=1, keepdims=True))
    alpha = jnp.exp(m_old - m_new)
    l_new = alpha * l_ref[...]
    acc = alpha * acc_ref[...]
    for j in range(pg):
        pr = jnp.exp(s[j] - m_new)
        l_new = l_new + jnp.sum(pr, axis=1, keepdims=True)
        acc = acc + _dot(pr.astype(BF16), vc_refs[j][0].reshape(n_rows, LANES).astype(BF16))
    l_ref[...] = l_new
    acc_ref[...] = acc
    m_ref[...] = m_new

    @pl.when(p == pl.num_programs(1) - 1)
    def _():
        lam = _lambda_value(lam_ref, lam_init)
        o = (acc_ref[0:heads, :] / l_ref[0:heads, :]
             - lam * (acc_ref[heads:2 * heads, :] / l_ref[heads:2 * heads, :]))
        o = o * lax.rsqrt(jnp.mean(o * o, axis=-1, keepdims=True) + RMS_EPS) * w_ref[...] * (1.0 - lam_init)
        o_ref[0] = o.astype(o_ref.dtype)


def attn_decode(q, k_new, v_new, cache_k, cache_v, page_table, lam_vec, subln_w, *, lam_init):
    b, heads, _ = q.shape
    n_pages = page_table.shape[1]
    page = cache_k.shape[1]
    pg = math.gcd(n_pages, DECODE_PAGES_PER_STEP)
    tok = pl.BlockSpec((1, heads, LANES), lambda i, p, pt: (i, 0, 0))

    def cache(j):
        return pl.BlockSpec((1, page, heads, LANES), lambda i, p, pt: (pt[i * n_pages + p * pg + j], 0, 0, 0))

    grid_spec = pltpu.PrefetchScalarGridSpec(
        num_scalar_prefetch=1,
        grid=(b, n_pages // pg),
        in_specs=[pl.BlockSpec(lam_vec.shape, lambda i, p, pt: (0, 0)), tok, tok, tok]
        + [cache(j) for j in range(pg)] * 2
        + [pl.BlockSpec((1, LANES), lambda i, p, pt: (0, 0))],
        out_specs=tok,
        scratch_shapes=[pltpu.VMEM((2 * heads, 1), F32), pltpu.VMEM((2 * heads, 1), F32),
                        pltpu.VMEM((2 * heads, LANES), F32)],
    )
    return pl.pallas_call(
        functools.partial(_attn_decode_kernel, lam_init=lam_init, pg=pg),
        out_shape=jax.ShapeDtypeStruct(q.shape, BF16),
        grid_spec=grid_spec,
        compiler_params=_params("parallel", "arbitrary"),
        name="attn_decode",
    )(page_table.reshape(-1), lam_vec, q, k_new, v_new, *([cache_k] * pg), *([cache_v] * pg),
      subln_w.reshape(1, LANES))


def _router_kernel(x_ref, sc_ref, sh_ref, wr_ref, br_ref, idx_ref, gate_ref, pos_ref, cnt_ref,
                   tri_ref, carry_ref, *, tm):
    i = pl.program_id(0)

    @pl.when(i == 0)
    def _():
        r = lax.broadcasted_iota(I32, (tm, tm), 0)
        c = lax.broadcasted_iota(I32, (tm, tm), 1)
        tri_ref[...] = jnp.where(c < r, 1.0, 0.0).astype(BF16)
        carry_ref[...] = jnp.zeros_like(carry_ref)

    h = x_ref[...] * (1.0 + sc_ref[0]) + sh_ref[0]
    h_hi = h.astype(BF16)
    h_lo = (h - h_hi.astype(F32)).astype(BF16)
    w = wr_ref[...]
    w_hi = w.astype(BF16)
    w_lo = (w - w_hi.astype(F32)).astype(BF16)
    logits = _dot(h_hi, w_hi) + _dot(h_lo, w_hi) + _dot(h_hi, w_lo) + br_ref[...]
    lane = lax.broadcasted_iota(I32, (1, LANES), 1)
    work = jnp.where(lane < N_EXPERTS, logits, NEG_BIG)
    tops, idxs = [], []
    member = jnp.zeros((tm, LANES), F32)
    for _ in range(TOP_K):
        mx = jnp.max(work, axis=1, keepdims=True)
        ix = jnp.min(jnp.where(work == mx, lane, LANES), axis=1, keepdims=True)
        hit = lane == ix
        member = jnp.where(hit, 1.0, member)
        work = jnp.where(hit, NEG_BIG, work)
        tops.append(mx)
        idxs.append(ix)
    ex = [jnp.exp(t - tops[0]) for t in tops]
    denom = ex[0] + ex[1] + ex[2] + ex[3]
    rank = _dot(tri_ref[...], member.astype(BF16)) + carry_ref[...]
    idx_out = jnp.zeros((tm, LANES), I32)
    gate_out = jnp.zeros((tm, LANES), F32)
    pos_out = jnp.zeros((tm, LANES), I32)
    for k in range(TOP_K):
        pk = jnp.sum(jnp.where(lane == idxs[k], rank, 0.0), axis=1, keepdims=True)
        idx_out = jnp.where(lane == k, idxs[k], idx_out)
        gate_out = jnp.where(lane == k, ex[k] / denom, gate_out)
        pos_out = jnp.where(lane == k, pk.astype(I32), pos_out)
    idx_ref[...] = idx_out
    gate_ref[...] = gate_out
    pos_ref[...] = pos_out
    carry_ref[...] = carry_ref[...] + jnp.sum(member, axis=0, keepdims=True)
    cnt_ref[...] = carry_ref[...].astype(I32)


def router(x, sc, sh, w_r, b_r, *, rows_per_group, tm):
    n, d = x.shape
    tiles_per_group = rows_per_group // tm
    r = sc.shape[1]
    w_pad = jnp.zeros((d, LANES), F32).at[:, :N_EXPERTS].set(w_r)
    b_pad = jnp.zeros((1, LANES), F32).at[0, :N_EXPERTS].set(b_r)
    mod = pl.BlockSpec((1, r, d), lambda i: (i // tiles_per_group, 0, 0))
    wide = pl.BlockSpec((tm, LANES), lambda i: (i, 0))
    return pl.pallas_call(
        functools.partial(_router_kernel, tm=tm),
        out_shape=(jax.ShapeDtypeStruct((n, LANES), I32), jax.ShapeDtypeStruct((n, LANES), F32),
                   jax.ShapeDtypeStruct((n, LANES), I32), jax.ShapeDtypeStruct((1, LANES), I32)),
        grid=(n // tm,),
        in_specs=[pl.BlockSpec((tm, d), lambda i: (i, 0)), mod, mod,
                  pl.BlockSpec((d, LANES), lambda i: (0, 0)), pl.BlockSpec((1, LANES), lambda i: (0, 0))],
        out_specs=(wide, wide, wide, pl.BlockSpec((1, LANES), lambda i: (0, 0))),
        scratch_shapes=[pltpu.VMEM((tm, tm), BF16), pltpu.VMEM((1, LANES), F32)],
        compiler_params=_params("arbitrary"),
        name="moe_router",
    )(x, sc, sh, w_pad, b_pad)


def _dispatch_kernel(gap_lo_ref, gap_hi_ref, x_ref, sc_ref, sh_ref, dest_ref, xs_ref,
                     h_ref, zero_ref, idx_ref, sem_ref, isem_ref, *, tm, n_gap_rows):
    i = pl.program_id(0)
    n_idx = tm * TOP_K
    idx_copy = pltpu.make_async_copy(dest_ref.at[0, 0], idx_ref, isem_ref)
    idx_copy.start()
    h = x_ref[...] * (1.0 + sc_ref[0]) + sh_ref[0]
    for g in range(h.shape[1] // LANES):
        h_ref[:, g, :] = h[:, g * LANES:(g + 1) * LANES]
    idx_copy.wait()

    def row_copy(src_row, dst_row):
        return pltpu.make_async_copy(src_row, xs_ref.at[pl.ds(dst_row, 1)], sem_ref)

    def issue(r, carry):
        for k in range(TOP_K):
            row_copy(h_ref.at[pl.ds(r, 1)], idx_ref[r * TOP_K + k]).start()
        return carry

    lax.fori_loop(0, tm, issue, 0, unroll=DMA_UNROLL)

    def drain(r, carry):
        row_copy(h_ref.at[pl.ds(0, 1)], 0).wait()
        return carry

    lax.fori_loop(0, n_idx, drain, 0, unroll=DMA_UNROLL)

    @pl.when(i == 0)
    def _():
        zero_ref[...] = jnp.zeros_like(zero_ref)

        def per_expert(e, carry):
            def fill(rw, c2):
                row_copy(zero_ref.at[pl.ds(0, 1)], rw).start()
                return c2
            return lax.fori_loop(gap_lo_ref[e], gap_hi_ref[e], fill, carry)

        lax.fori_loop(0, N_EXPERTS, per_expert, 0)

        def drain_zero(r, carry):
            row_copy(zero_ref.at[pl.ds(0, 1)], 0).wait()
            return carry

        lax.fori_loop(0, n_gap_rows, drain_zero, 0)


def dispatch(x, sc, sh, dest, gap_lo, gap_hi, *, rows_per_group, tm, n_slots_padded):
    n, d = x.shape
    tiles_per_group = rows_per_group // tm
    r = sc.shape[1]
    n_idx = tm * TOP_K
    mod = pl.BlockSpec((1, r, d), lambda i, lo, hi: (i // tiles_per_group, 0, 0))
    grid_spec = pltpu.PrefetchScalarGridSpec(
        num_scalar_prefetch=2,
        grid=(n // tm,),
        in_specs=[pl.BlockSpec((tm, d), lambda i, lo, hi: (i, 0)), mod, mod,
                  pl.BlockSpec((1, 1, n_idx), lambda i, lo, hi: (i, 0, 0))],
        out_specs=pl.BlockSpec(memory_space=pl.ANY),
        scratch_shapes=[pltpu.VMEM((tm, d // LANES, LANES), F32), pltpu.VMEM((1, d // LANES, LANES), F32),
                        pltpu.SMEM((n_idx,), I32), pltpu.SemaphoreType.DMA, pltpu.SemaphoreType.DMA],
    )
    return pl.pallas_call(
        functools.partial(_dispatch_kernel, tm=tm, n_gap_rows=n_slots_padded - n * TOP_K),
        out_shape=jax.ShapeDtypeStruct((n_slots_padded, d // LANES, LANES), F32),
        grid_spec=grid_spec,
        compiler_params=_params("arbitrary"),
        name="moe_dispatch",
    )(gap_lo, gap_hi, x, sc, sh, dest.reshape(n // tm, 1, n_idx))


def _prep_expert_kernel(wgu_ref, wd_ref, wgu_out, wd_out):
    half = GU_GROUP // 2
    r = lax.broadcasted_iota(I32, (GU_GROUP, GU_GROUP), 0)
    c = lax.broadcasted_iota(I32, (GU_GROUP, GU_GROUP), 1)
    perm = jnp.where(r == jnp.where(c < half, 2 * c, 2 * (c - half) + 1), 1.0, 0.0).astype(BF16)
    for g in range(wgu_ref.shape[2] // GU_GROUP):
        cols = slice(g * GU_GROUP, (g + 1) * GU_GROUP)
        wgu_out[0, :, cols] = _dot(wgu_ref[0, :, cols].astype(BF16), perm).astype(BF16)
    wd_out[0] = wd_ref[0].astype(BF16)


def prep_expert_weights(w_gu, w_down):
    ne, d, de2 = w_gu.shape
    gu_spec = pl.BlockSpec((1, d, de2), lambda e: (e, 0, 0))
    dn_spec = pl.BlockSpec((1, de2 // 2, d), lambda e: (e, 0, 0))
    return pl.pallas_call(
        _prep_expert_kernel,
        out_shape=(jax.ShapeDtypeStruct(w_gu.shape, BF16), jax.ShapeDtypeStruct(w_down.shape, BF16)),
        grid=(ne,),
        in_specs=[gu_spec, dn_spec],
        out_specs=(gu_spec, dn_spec),
        compiler_params=_params("parallel"),
        name="moe_prep_weights",
    )(w_gu, w_down)


def _expert_ffn_kernel(be_ref, xs_ref, wgu_ref, bgu_ref, wd_ref, bd_ref, ys_ref):
    n_lane_groups = xs_ref.shape[1]
    x = jnp.concatenate([xs_ref[:, g, :] for g in range(n_lane_groups)], axis=1).astype(BF16)
    gu = _dot(x, wgu_ref[0]) + bgu_ref[0]
    half = GU_GROUP // 2
    acts = []
    for g in range(gu.shape[1] // GU_GROUP):
        gate = jnp.minimum(gu[:, g * GU_GROUP:g * GU_GROUP + half], SWIGLU_LIMIT)
        up = jnp.clip(gu[:, g * GU_GROUP + half:(g + 1) * GU_GROUP], -SWIGLU_LIMIT, SWIGLU_LIMIT)
        acts.append((up + 1.0) * (gate * _sigmoid(gate * SWIGLU_ALPHA)))
    act = jnp.concatenate(acts, axis=1).astype(BF16)
    y = _dot(act, wd_ref[0]) + bd_ref[0]
    for g in range(n_lane_groups):
        ys_ref[:, g, :] = y[:, g * LANES:(g + 1) * LANES]


def expert_ffn(xs, block_expert, wgu, bgu, wd, bd, *, blk):
    ns, lg, _ = xs.shape
    d, de2 = wgu.shape[1], wgu.shape[2]
    rows = pl.BlockSpec((blk, lg, LANES), lambda i, be: (i, 0, 0))
    wspec = lambda a, b: pl.BlockSpec((1, a, b), lambda i, be: (be[i], 0, 0))
    grid_spec = pltpu.PrefetchScalarGridSpec(
        num_scalar_prefetch=1,
        grid=(ns // blk,),
        in_specs=[rows, wspec(d, de2), wspec(1, de2), wspec(de2 // 2, d), wspec(1, d)],
        out_specs=rows,
    )
    return pl.pallas_call(
        _expert_ffn_kernel,
        out_shape=jax.ShapeDtypeStruct(xs.shape, F32),
        grid_spec=grid_spec,
        compiler_params=_params("arbitrary"),
        name="moe_expert_ffn",
    )(block_expert, xs, wgu, bgu, wd, bd)


def _combine_ln_kernel(x_ref, gm_ref, gate_ref, dest_ref, ys_ref, lg_ref, lb_ref, o_ref,
                       buf_ref, idx_ref, sem_ref, isem_ref, *, tm, alpha):
    n_idx = tm * TOP_K
    idx_copy = pltpu.make_async_copy(dest_ref.at[0, 0], idx_ref, isem_ref)
    idx_copy.start()
    idx_copy.wait()

    def row_copy(src_row, k, r):
        return pltpu.make_async_copy(ys_ref.at[pl.ds(src_row, 1)], buf_ref.at[k, pl.ds(r, 1)], sem_ref)

    def issue(r, carry):
        for k in range(TOP_K):
            row_copy(idx_ref[r * TOP_K + k], k, r).start()
        return carry

    lax.fori_loop(0, tm, issue, 0, unroll=DMA_UNROLL)

    def drain(r, carry):
        row_copy(0, 0, 0).wait()
        return carry

    lax.fori_loop(0, n_idx, drain, 0, unroll=DMA_UNROLL)
    gates = gate_ref[...]
    cols = []
    for g in range(buf_ref.shape[2]):
        yg = gates[:, 0:1] * buf_ref[0, :, g, :]
        for k in range(1, TOP_K):
            yg = yg + gates[:, k:k + 1] * buf_ref[k, :, g, :]
        cols.append(yg)
    y = jnp.concatenate(cols, axis=1)
    o_ref[...] = _layer_norm(alpha * x_ref[...] + gm_ref[0] * y, lg_ref[...], lb_ref[...])


def combine_ln(x, gm, gates, dest, ys, ln_g, ln_b, *, rows_per_group, tm, alpha):
    n, d = x.shape
    tiles_per_group = rows_per_group // tm
    r = gm.shape[1]
    n_idx = tm * TOP_K
    row = pl.BlockSpec((tm, d), lambda i: (i, 0))
    vec = pl.BlockSpec((1, d), lambda i: (0, 0))
    return pl.pallas_call(
        functools.partial(_combine_ln_kernel, tm=tm, alpha=alpha),
        out_shape=jax.ShapeDtypeStruct((n, d), F32),
        grid=(n // tm,),
        in_specs=[row, pl.BlockSpec((1, r, d), lambda i: (i // tiles_per_group, 0, 0)),
                  pl.BlockSpec((tm, LANES), lambda i: (i, 0)),
                  pl.BlockSpec((1, 1, n_idx), lambda i: (i, 0, 0)),
                  pl.BlockSpec(memory_space=pl.ANY), vec, vec],
        out_specs=row,
        scratch_shapes=[pltpu.VMEM((TOP_K, tm, d // LANES, LANES), F32), pltpu.SMEM((n_idx,), I32),
                        pltpu.SemaphoreType.DMA, pltpu.SemaphoreType.DMA],
        compiler_params=_params("arbitrary"),
        name="moe_combine_ln",
    )(x, gm, gates, dest.reshape(n // tm, 1, n_idx), ys, ln_g.reshape(1, d), ln_b.reshape(1, d))


def _moe_layer(x, sc, sh, gm, ln_g, ln_b, w_r, b_r, wgu, bgu, wd, bd, *, rows_per_group, tm, blk, alpha):
    n, d = x.shape
    idx_w, gate_w, pos_w, cnt_w = router(x, sc, sh, w_r, b_r, rows_per_group=rows_per_group, tm=tm)
    counts = cnt_w[0, :N_EXPERTS]
    top_idx, pos = idx_w[:, :TOP_K], pos_w[:, :TOP_K]
    n_slots = n * TOP_K
    n_blocks = -(-(n_slots + N_EXPERTS * (blk - 1)) // blk)
    padded = (counts + blk - 1) // blk * blk
    pad_end = jnp.cumsum(padded)
    pad_start = pad_end - padded
    onehot = top_idx[..., None] == jnp.arange(N_EXPERTS, dtype=I32)
    dest = (jnp.sum(jnp.where(onehot, pad_start, 0), axis=-1) + pos).astype(I32)
    block_start = jnp.arange(n_blocks, dtype=I32) * blk
    block_expert = jnp.minimum(
        jnp.sum((pad_end[None, :] <= block_start[:, None]).astype(I32), axis=1), N_EXPERTS - 1).astype(I32)
    gap_lo = (pad_start + counts).astype(I32)
    gap_hi = jnp.concatenate([pad_start[1:], jnp.full((1,), n_blocks * blk, I32)]).astype(I32)
    xs = dispatch(x, sc, sh, dest, gap_lo, gap_hi, rows_per_group=rows_per_group, tm=tm,
                  n_slots_padded=n_blocks * blk)
    ys = expert_ffn(xs, block_expert, wgu, bgu, wd, bd, blk=blk)
    return combine_ln(x, gm, gate_w, dest, ys, ln_g, ln_b, rows_per_group=rows_per_group, tm=tm, alpha=alpha)


def _rope_tables(pos):
    inv_freq = ROPE_THETA ** (-jnp.arange(ROPE_HALF, dtype=F32) / ROPE_HALF)
    ang = pos.astype(F32)[:, None] * inv_freq[None, :]
    cos, sin = jnp.cos(ang), jnp.sin(ang)
    reps = LANES // DIFF_HEAD_DIM
    return (jnp.concatenate([cos, cos] * reps, axis=1), jnp.concatenate([-sin, sin] * reps, axis=1))


def _trunk(x, mods, kv_mods, past_len, s0, cache, wts, *, batch, seq, tm, blk):
    n, d = x.shape
    depth = wts["ada_w"].shape[0]
    n_a = wts["hgrn_w_in"].shape[0]
    alpha = (2 * depth) ** 0.25
    rows_per_group = seq if seq >= tm else n
    kw = dict(rows_per_group=rows_per_group, tm=tm)
    decode = seq == 1
    pos = past_len + (jnp.zeros((n,), I32) if decode else jnp.arange(seq, dtype=I32))
    rope = _rope_tables(pos)
    heads = d // LANES
    states = []
    k_new = v_new = k16 = v16 = None
    for l in range(depth):
        sh_m, sc_m, g_m, sh_f, sc_f, g_f = mods[l]
        if l < n_a:
            proj, = modmm(x, sc_m, sh_m, wts["hgrn_w_in"][l], out_dtypes=(F32,), **kw)
            lbs = [a[l][None] for a in wts["lb"]]
            nw = wts["hgrn_norm_w"][l].reshape(1, LANES)
            if decode:
                o, s_l = hgrn_step(proj, *lbs, nw, s0[l])
            else:
                o, s_l = hgrn_chunked(proj, *lbs, nw, s0[l], batch=batch, seq=seq, tb=min(seq, 512))
            states.append(s_l)
            w_out = wts["hgrn_w_out"][l]
        else:
            j = l - n_a
            lam_init = 0.8 - 0.6 * math.exp(-0.3 * l)
            if l == n_a:
                kv_shift, kv_scale = kv_mods
                half = wts["kv_w"].shape[1] // 2
                k_new, k16 = modmm(x, kv_scale, kv_shift, wts["kv_w"][:, :half], out_dtypes=(F32, BF16),
                                   rope=rope, **kw)
                v_new, v16 = modmm(x, kv_scale, kv_shift, wts["kv_w"][:, half:], out_dtypes=(F32, BF16), **kw)
            q_scale = DIFF_HEAD_DIM ** -0.5
            if decode:
                q, = modmm(x, sc_m, sh_m, wts["diff_w_q"][j], out_dtypes=(F32,), rope=rope, out_scale=q_scale, **kw)
                o = attn_decode(q.reshape(n, heads, LANES), k_new.reshape(n, heads, LANES),
                                v_new.reshape(n, heads, LANES), cache[0], cache[1], cache[2],
                                wts["diff_lambda"][j], wts["diff_subln_w"][j], lam_init=lam_init)
                o = o.reshape(n, d)
            else:
                q, = modmm(x, sc_m, sh_m, wts["diff_w_q"][j], out_dtypes=(BF16,), rope=rope,
                           out_scale=q_scale * math.log2(math.e), **kw)
                o = attn_prefill(q, k16, v16, wts["diff_lambda"][j], wts["diff_subln_w"][j],
                                 batch=batch, seq=seq, tq=min(seq, 512), lam_init=lam_init)
            w_out = wts["diff_w_out"][j]
        x = proj_ln(o, w_out, x, g_m, wts["ln_g"][l, 0], wts["ln_b"][l, 0], alpha=alpha, **kw)
        x = _moe_layer(x, sc_f, sh_f, g_f, wts["ln_g"][l, 1], wts["ln_b"][l, 1],
                       wts["moe_w_router"][l], wts["moe_b_router"][l],
                       wts["moe_w_gu"][l], wts["moe_b_gu"][l], wts["moe_w_down"][l], wts["moe_b_down"][l],
                       alpha=alpha, blk=blk, **kw)
    return x, k_new, v_new, jnp.stack(states)


def kernel(x_prompt, x_sample, cache_k, cache_v, state_hgrn, page_table, c_prompt, c_sample, ada_w, ada_b, ln_g, ln_b, hgrn_w_in, hgrn_lb_logits, hgrn_norm_w, hgrn_w_out, kv_ada_w, kv_ada_b, kv_w, diff_w_q, diff_lambda, diff_subln_w, diff_w_out, moe_w_router, moe_b_router, moe_w_gu, moe_b_gu, moe_w_down, moe_b_down):
    b_p, t_p, d = x_prompt.shape
    b_d, t_d, _ = x_sample.shape
    depth = ada_w.shape[0]
    n_a = hgrn_w_in.shape[0]
    heads = d // LANES
    de = moe_w_down.shape[2]

    m_rows = -(-(b_d + b_p) // 8) * 8
    c_all = jnp.zeros((m_rows, d), F32).at[:b_d].set(c_sample).at[b_d:b_d + b_p].set(c_prompt)
    mod_all = adaln(c_all, ada_w, ada_b)
    kv_mod_all = adaln(c_all, kv_ada_w[None], kv_ada_b[None])[0]

    def split_mods(m, n_vec, lo, hi, per_row):
        parts = [m[lo:hi, v * d:(v + 1) * d] for v in range(n_vec)]
        return [p[None] if per_row else p[:, None, :] for p in parts]

    mods_p = [split_mods(mod_all[l], 6, b_d, b_d + b_p, False) for l in range(depth)]
    mods_s = [split_mods(mod_all[l], 6, 0, b_d, True) for l in range(depth)]
    kv_mods_p = split_mods(kv_mod_all, 2, b_d, b_d + b_p, False)
    kv_mods_s = split_mods(kv_mod_all, 2, 0, b_d, True)

    loglb, log1m, onem = hgrn_lower_bounds(hgrn_lb_logits)

    n_exp = moe_w_gu.shape[1]
    w_gu16, w_down16 = prep_expert_weights(moe_w_gu.reshape(depth * n_exp, d, 2 * de),
                                           moe_w_down.reshape(depth * n_exp, de, d))
    half = GU_GROUP // 2
    b_gu = moe_b_gu.reshape(depth, n_exp, 2 * de // GU_GROUP, half, 2)
    b_gu = jnp.swapaxes(b_gu, -1, -2).reshape(depth, n_exp, 1, 2 * de)

    wts = dict(
        ada_w=ada_w, ln_g=ln_g, ln_b=ln_b,
        hgrn_w_in=hgrn_w_in.astype(BF16), lb=(loglb, log1m, onem), hgrn_norm_w=hgrn_norm_w,
        hgrn_w_out=hgrn_w_out.astype(BF16), kv_w=kv_w.astype(BF16), diff_w_q=diff_w_q.astype(BF16),
        diff_lambda=diff_lambda, diff_subln_w=diff_subln_w, diff_w_out=diff_w_out.astype(BF16),
        moe_w_router=moe_w_router, moe_b_router=moe_b_router,
        moe_w_gu=w_gu16.reshape(depth, n_exp, d, 2 * de), moe_b_gu=b_gu,
        moe_w_down=w_down16.reshape(depth, n_exp, de, d), moe_b_down=moe_b_down.reshape(depth, n_exp, 1, d),
    )

    s0_prompt = jnp.zeros((n_a, b_p, heads, LANES, LANES), F32)
    y_p, k_p, v_p, st_p = _trunk(x_prompt.reshape(b_p * t_p, d), mods_p, kv_mods_p, 0, s0_prompt, None, wts,
                                 batch=b_p, seq=t_p, tm=min(512, t_p), blk=256)
    past_len = page_table.shape[1] * cache_k.shape[1]
    y_s, k_s, v_s, st_s = _trunk(x_sample.reshape(b_d * t_d, d), mods_s, kv_mods_s, past_len, state_hgrn,
                                 (cache_k, cache_v, page_table), wts, batch=b_d, seq=t_d, tm=b_d, blk=16)
    dv = d // heads
    return (y_p.reshape(b_p, t_p, d), y_s.reshape(b_d, t_d, d),
            k_p.reshape(b_p, t_p, heads, dv), v_p.reshape(b_p, t_p, heads, dv), st_p,
            k_s.reshape(b_d, t_d, heads, dv), v_s.reshape(b_d, t_d, heads, dv), st_s)
```

```python
import functools
import math

import jax
import jax.numpy as jnp
from jax import lax
from jax.experimental import pallas as pl
from jax.experimental.pallas import tpu as pltpu

F32 = jnp.float32
BF16 = jnp.bfloat16
I32 = jnp.int32

LANES = 128
HGRN_CHUNK = 64
HGRN_SUB = 16
LB_FLOOR = 1e-30
DIFF_HEAD_DIM = 64
ROPE_THETA = 10000.0
ROPE_HALF = DIFF_HEAD_DIM // 2
NEG_BIG = -1e30
N_EXPERTS = 32
TOP_K = 4
SWIGLU_LIMIT = 7.0
SWIGLU_ALPHA = 1.702
LN_EPS = 1e-5
RMS_EPS = 1e-5
VMEM_LIMIT = 56 * 1024 * 1024
DECODE_PAGES_PER_STEP = 8
GU_GROUP = 256
DMA_UNROLL = 8


def _params(*sem):
    return pltpu.CompilerParams(dimension_semantics=sem, vmem_limit_bytes=VMEM_LIMIT)


def _sigmoid(x):
    return 1.0 / (1.0 + jnp.exp(-x))


def _silu(x):
    return x * _sigmoid(x)


def _log_sigmoid(x):
    return jnp.minimum(x, 0.0) - jnp.log1p(jnp.exp(-jnp.abs(x)))


def _layer_norm(z, g, b):
    mu = jnp.mean(z, axis=-1, keepdims=True)
    zc = z - mu
    var = jnp.mean(zc * zc, axis=-1, keepdims=True)
    return zc * lax.rsqrt(var + LN_EPS) * g + b


def _dot(a, b):
    return jnp.dot(a, b, preferred_element_type=F32)


def _dot_nt(a, b):
    return lax.dot_general(a, b, (((1,), (1,)), ((), ())), preferred_element_type=F32)


def _dot_tn(a, b):
    return lax.dot_general(a, b, (((0,), (0,)), ((), ())), preferred_element_type=F32)


def _token_rows(ref, tok, lg):
    start = tok * lg
    return ref.at[pl.ds(start if isinstance(start, int) else pl.multiple_of(start, lg), lg)]


def _split3(x):
    hi = x.astype(BF16)
    r1 = x - hi.astype(F32)
    mid = r1.astype(BF16)
    lo = (r1 - mid.astype(F32)).astype(BF16)
    return hi, mid, lo


def _adaln_kernel(c_ref, w_ref, b_ref, o_ref):
    s = _silu(c_ref[...]).astype(BF16)
    o_ref[0] = _dot(s, w_ref[0].astype(BF16)) + b_ref[0]


def adaln(c, w, b, tn=1024):
    m, d = c.shape
    nl, _, no = w.shape
    return pl.pallas_call(
        _adaln_kernel,
        out_shape=jax.ShapeDtypeStruct((nl, m, no), F32),
        grid=(nl, no // tn),
        in_specs=[
            pl.BlockSpec((m, d), lambda l, j: (0, 0)),
            pl.BlockSpec((1, d, tn), lambda l, j: (l, 0, j)),
            pl.BlockSpec((1, 1, tn), lambda l, j: (l, 0, j)),
        ],
        out_specs=pl.BlockSpec((1, m, tn), lambda l, j: (l, 0, j)),
        compiler_params=_params("parallel", "parallel"),
        name="adaln",
    )(c, w, b.reshape(nl, 1, no))


def _lb_kernel(logit_ref, loglb_ref, log1m_ref, onem_ref):
    x = logit_ref[...]
    nl = x.shape[0]
    e = jnp.exp(x - jnp.max(x, axis=0, keepdims=True))
    sm = e / jnp.sum(e, axis=0, keepdims=True)
    acc = jnp.zeros_like(sm[0:1])
    for l in range(nl):
        lb = jnp.clip(acc, 0.0, 1.0 - 1e-6)
        loglb_ref[l] = jnp.log(jnp.maximum(lb, LB_FLOOR))
        log1m_ref[l] = jnp.log1p(-lb)
        onem_ref[l] = 1.0 - lb
        acc = acc + sm[l:l + 1]


def hgrn_lower_bounds(lb_logits):
    nl, f = lb_logits.shape
    shp = jax.ShapeDtypeStruct((nl, 1, f), F32)
    return pl.pallas_call(_lb_kernel, out_shape=(shp, shp, shp), name="hgrn_lb")(lb_logits)


def _rotary_tile(y, cos, sin_signed):
    lane = lax.broadcasted_iota(I32, (1, LANES), 1)
    first_half = (lane % DIFF_HEAD_DIM) < ROPE_HALF
    partner = jnp.where(first_half, pltpu.roll(y, LANES - ROPE_HALF, 1), pltpu.roll(y, ROPE_HALF, 1))
    return y * cos + partner * sin_signed


def _modmm_kernel(*refs, rotary, out_scale, n_out):
    if rotary:
        x_ref, sc_ref, sh_ref, w_ref, cos_ref, sin_ref = refs[:6]
        rest = refs[6:]
    else:
        x_ref, sc_ref, sh_ref, w_ref = refs[:4]
        rest = refs[4:]
    out_refs, h_ref = rest[:n_out], rest[n_out]

    @pl.when(pl.program_id(1) == 0)
    def _():
        h_ref[...] = (x_ref[...] * (1.0 + sc_ref[0]) + sh_ref[0]).astype(BF16)

    y = _dot(h_ref[...], w_ref[...])
    if rotary:
        cos, sin = cos_ref[...], sin_ref[...]
        y = jnp.concatenate(
            [_rotary_tile(y[:, g * LANES:(g + 1) * LANES], cos, sin) for g in range(y.shape[1] // LANES)], axis=1)
    if out_scale != 1.0:
        y = y * out_scale
    for o_ref in out_refs:
        o_ref[...] = y.astype(o_ref.dtype)


def modmm(x, sc, sh, w, *, rows_per_group, tm, out_dtypes, rope=None, out_scale=1.0, tn=1024):
    n, d = x.shape
    no = w.shape[1]
    tiles_per_group = rows_per_group // tm
    r = sc.shape[1]
    in_specs = [
        pl.BlockSpec((tm, d), lambda i, j: (i, 0)),
        pl.BlockSpec((1, r, d), lambda i, j: (i // tiles_per_group, 0, 0)),
        pl.BlockSpec((1, r, d), lambda i, j: (i // tiles_per_group, 0, 0)),
        pl.BlockSpec((d, tn), lambda i, j: (0, j)),
    ]
    args = [x, sc, sh, w]
    if rope is not None:
        pos_tiles = rope[0].shape[0] // tm
        in_specs += [pl.BlockSpec((tm, LANES), lambda i, j: (i % pos_tiles, 0))] * 2
        args += list(rope)
    outs = tuple(jax.ShapeDtypeStruct((n, no), dt) for dt in out_dtypes)
    res = pl.pallas_call(
        functools.partial(_modmm_kernel, rotary=rope is not None, out_scale=out_scale, n_out=len(outs)),
        out_shape=outs,
        grid=(n // tm, no // tn),
        in_specs=in_specs,
        out_specs=tuple(pl.BlockSpec((tm, tn), lambda i, j: (i, j)) for _ in outs),
        scratch_shapes=[pltpu.VMEM((tm, d), BF16)],
        compiler_params=_params("parallel", "arbitrary"),
        name="modmm",
    )(*args)
    return res


def _hgrn_head_chunk(q_raw, f, v, g_raw, loglb, log1m, onem, nw, st, tri, sub_row):
    c, sub = HGRN_CHUNK, HGRN_SUB
    q = _silu(q_raw)
    x2 = log1m + _log_sigmoid(f)
    mx = jnp.maximum(loglb, x2)
    logf = mx + jnp.log1p(jnp.exp(jnp.minimum(loglb, x2) - mx))
    kk = onem * _sigmoid(-f)
    hi, mid, lo = _split3(logf)
    b = _dot(tri, hi) + _dot(tri, mid) + _dot(tri, lo)
    v16 = v.astype(BF16)
    o_inter = _dot_nt((q * jnp.exp(b)).astype(BF16), st.astype(BF16))
    o_parts = []
    for i in range(c // sub):
        sl = slice(i * sub, (i + 1) * sub)
        qi, bi, ki, vi = q[sl], b[sl], kk[sl], v[sl]
        oi = o_inter[sl]
        for s in range(sub):
            w = qi * ki[s:s + 1] * jnp.exp(jnp.minimum(bi - bi[s:s + 1], 0.0))
            a = jnp.sum(w, axis=1, keepdims=True)
            a = jnp.where(sub_row >= s, a, 0.0)
            oi = oi + a * vi[s:s + 1]
        if i > 0:
            r = b[i * sub - 1:i * sub]
            qs = (qi * jnp.exp(bi - r)).astype(BF16)
            ks = (kk[:i * sub] * jnp.exp(r - b[:i * sub])).astype(BF16)
            a = _dot_nt(qs, ks)
            oi = oi + _dot(a.astype(BF16), v16[:i * sub])
        o_parts.append(oi)
    o = jnp.concatenate(o_parts, axis=0)
    b_last = b[c - 1:c]
    kd = (kk * jnp.exp(b_last - b)).astype(BF16)
    st_new = st * jnp.exp(b_last) + _dot_tn(v16, kd)
    o = o * lax.rsqrt(jnp.mean(o * o, axis=-1, keepdims=True) + RMS_EPS) * nw * _silu(g_raw)
    return o, st_new


def _hgrn_chunk_kernel(q_ref, f_ref, v_ref, g_ref, loglb_ref, log1m_ref, onem_ref, nw_ref, s0_ref,
                       o_ref, sfin_ref, st_ref, *, n_chunks, hg):
    t = pl.program_id(2)
    c = HGRN_CHUNK

    @pl.when(t == 0)
    def _():
        for hh in range(hg):
            st_ref[hh] = s0_ref[0, hh].T

    loglb, log1m, onem, nw = loglb_ref[0], log1m_ref[0], onem_ref[0], nw_ref[...]
    row = lax.broadcasted_iota(I32, (c, c), 0)
    col = lax.broadcasted_iota(I32, (c, c), 1)
    tri = jnp.where(col <= row, 1.0, 0.0).astype(BF16)
    sub_row = lax.broadcasted_iota(I32, (HGRN_SUB, 1), 0)

    def chunk(ci, carry):
        rows = pl.ds(pl.multiple_of(ci * c, c), c)
        outs = []
        for hh in range(hg):
            ln = slice(hh * LANES, (hh + 1) * LANES)
            o, st_new = _hgrn_head_chunk(q_ref[rows, ln], f_ref[rows, ln], v_ref[rows, ln], g_ref[rows, ln],
                                         loglb[:, ln], log1m[:, ln], onem[:, ln], nw, st_ref[hh], tri, sub_row)
            st_ref[hh] = st_new
            outs.append(o)
        o_ref[rows, :] = jnp.concatenate(outs, axis=1).astype(o_ref.dtype)
        return carry

    lax.fori_loop(0, n_chunks, chunk, 0)

    @pl.when(t == pl.num_programs(2) - 1)
    def _():
        for hh in range(hg):
            sfin_ref[0, hh] = st_ref[hh].T


def hgrn_chunked(proj, loglb, log1m, onem, norm_w, s0, *, batch, seq, tb, hg=8):
    n = proj.shape[0]
    heads = proj.shape[1] // (4 * LANES)
    tpb = seq // tb
    ngrp = heads // hg
    w = hg * LANES
    row_blk = lambda off: pl.BlockSpec((tb, w), lambda b, h, t: (b * tpb + t, off * ngrp + h))
    vec = pl.BlockSpec((1, 1, w), lambda b, h, t: (0, 0, h))
    st_spec = pl.BlockSpec((1, hg, LANES, LANES), lambda b, h, t: (b, h, 0, 0))
    return pl.pallas_call(
        functools.partial(_hgrn_chunk_kernel, n_chunks=tb // HGRN_CHUNK, hg=hg),
        out_shape=(jax.ShapeDtypeStruct((n, heads * LANES), BF16),
                   jax.ShapeDtypeStruct(s0.shape, F32)),
        grid=(batch, ngrp, tpb),
        in_specs=[row_blk(0), row_blk(1), row_blk(2), row_blk(3), vec, vec, vec,
                  pl.BlockSpec((1, LANES), lambda b, h, t: (0, 0)), st_spec],
        out_specs=(pl.BlockSpec((tb, w), lambda b, h, t: (b * tpb + t, h)), st_spec),
        scratch_shapes=[pltpu.VMEM((hg, LANES, LANES), F32)],
        compiler_params=_params("parallel", "parallel", "arbitrary"),
        name="hgrn_chunked",
    )(proj, proj, proj, proj, loglb, log1m, onem, norm_w, s0)


def _hgrn_step_kernel(q_ref, f_ref, v_ref, g_ref, loglb_ref, log1m_ref, onem_ref, nw_ref, s_ref,
                      o_ref, snew_ref, *, bb):
    f = f_ref[...]
    q = _silu(q_ref[...])
    v = v_ref[...]
    x1 = loglb_ref[0]
    x2 = log1m_ref[0] + _log_sigmoid(f)
    mx = jnp.maximum(x1, x2)
    decay = jnp.exp(mx + jnp.log1p(jnp.exp(jnp.minimum(x1, x2) - mx)))
    kk = onem_ref[0] * _sigmoid(-f)
    q_t, d_t, k_t = q.T, decay.T, kk.T
    rows = []
    for e in range(bb):
        s_new = s_ref[e, 0] * d_t[:, e:e + 1] + k_t[:, e:e + 1] * v[e:e + 1, :]
        snew_ref[e, 0] = s_new
        rows.append(jnp.sum(q_t[:, e:e + 1] * s_new, axis=0, keepdims=True))
    o = jnp.concatenate(rows, axis=0)
    o = o * lax.rsqrt(jnp.mean(o * o, axis=-1, keepdims=True) + RMS_EPS) * nw_ref[...] * _silu(g_ref[...])
    o_ref[...] = o.astype(o_ref.dtype)


def hgrn_step(proj, loglb, log1m, onem, norm_w, state, *, bb=16):
    n = proj.shape[0]
    heads = proj.shape[1] // (4 * LANES)
    row_blk = lambda off: pl.BlockSpec((bb, LANES), lambda i, h: (i, off * heads + h))
    vec = pl.BlockSpec((1, 1, LANES), lambda i, h: (0, 0, h))
    st_spec = pl.BlockSpec((bb, 1, LANES, LANES), lambda i, h: (i, h, 0, 0))
    return pl.pallas_call(
        functools.partial(_hgrn_step_kernel, bb=bb),
        out_shape=(jax.ShapeDtypeStruct((n, heads * LANES), BF16),
                   jax.ShapeDtypeStruct(state.shape, F32)),
        grid=(n // bb, heads),
        in_specs=[row_blk(0), row_blk(1), row_blk(2), row_blk(3), vec, vec, vec,
                  pl.BlockSpec((1, LANES), lambda i, h: (0, 0)), st_spec],
        out_specs=(pl.BlockSpec((bb, LANES), lambda i, h: (i, h)), st_spec),
        compiler_params=_params("parallel", "parallel"),
        name="hgrn_step",
    )(proj, proj, proj, proj, loglb, log1m, onem, norm_w, state)


def _proj_ln_kernel(a_ref, w_ref, x_ref, gm_ref, lg_ref, lb_ref, o_ref, *, alpha):
    y = _dot(a_ref[...], w_ref[...])
    o_ref[...] = _layer_norm(alpha * x_ref[...] + gm_ref[0] * y, lg_ref[...], lb_ref[...])


def proj_ln(a, w, x, gm, ln_g, ln_b, *, rows_per_group, tm, alpha):
    n, d = x.shape
    tiles_per_group = rows_per_group // tm
    r = gm.shape[1]
    row = pl.BlockSpec((tm, d), lambda i: (i, 0))
    vec = pl.BlockSpec((1, d), lambda i: (0, 0))
    return pl.pallas_call(
        functools.partial(_proj_ln_kernel, alpha=alpha),
        out_shape=jax.ShapeDtypeStruct((n, d), F32),
        grid=(n // tm,),
        in_specs=[row, pl.BlockSpec((d, d), lambda i: (0, 0)), row,
                  pl.BlockSpec((1, r, d), lambda i: (i // tiles_per_group, 0, 0)), vec, vec],
        out_specs=row,
        compiler_params=_params("parallel"),
        name="proj_ln",
    )(a, w, x, gm, ln_g.reshape(1, d), ln_b.reshape(1, d))


def _lambda_value(lam_ref, lam_init):
    lv = lam_ref[...]
    s1 = jnp.sum(lv[0:1] * lv[1:2], axis=1, keepdims=True)
    s2 = jnp.sum(lv[2:3] * lv[3:4], axis=1, keepdims=True)
    return jnp.exp(s1) - jnp.exp(s2) + lam_init


def _attn_prefill_kernel(lam_ref, q_ref, k_ref, v_ref, w_ref, o_ref, *, tq, lam_init, hpb):
    i = pl.program_id(2)
    lane = lax.broadcasted_iota(I32, (1, LANES), 1)
    keep = lax.broadcasted_iota(I32, (tq, tq), 1) <= lax.broadcasted_iota(I32, (tq, tq), 0)
    q_maps = []
    for hh in range(hpb):
        q = q_ref[:, hh * LANES:(hh + 1) * LANES]
        zero = jnp.zeros_like(q)
        q_maps += [jnp.where(lane < DIFF_HEAD_DIM, q, zero), jnp.where(lane < DIFF_HEAD_DIM, zero, q)]

    def update(s, vj, m, l, acc):
        m_new = jnp.maximum(m, jnp.max(s, axis=1, keepdims=True))
        alpha = jnp.exp(m - m_new)
        p = jnp.exp(s - m_new)
        l = alpha * l + jnp.sum(p, axis=1, keepdims=True)
        acc = alpha * acc + _dot(p.astype(BF16), vj)
        return m_new, l, acc

    def block(j, carry, masked):
        rows = pl.ds(pl.multiple_of(j * tq, tq), tq)
        out = ()
        for hh in range(hpb):
            ln = slice(hh * LANES, (hh + 1) * LANES)
            kj, vj = k_ref[rows, ln], v_ref[rows, ln]
            for c in range(2):
                s = _dot_nt(q_maps[2 * hh + c], kj)
                if masked:
                    s = jnp.where(keep, s, NEG_BIG)
                st = 3 * (2 * hh + c)
                out += update(s, vj, *carry[st:st + 3])
        return out

    m0 = jnp.full((tq, 1), NEG_BIG, F32)
    l0 = jnp.zeros((tq, 1), F32)
    a0 = jnp.zeros((tq, LANES), F32)
    carry = lax.fori_loop(0, i, lambda j, c: block(j, c, False), (m0, l0, a0) * (2 * hpb))
    fin = block(i, carry, True)
    lam = _lambda_value(lam_ref, lam_init)
    outs = []
    for hh in range(hpb):
        _, l1, a1, _, l2, a2 = fin[6 * hh:6 * hh + 6]
        o = a1 / l1 - lam * (a2 / l2)
        outs.append(o * lax.rsqrt(jnp.mean(o * o, axis=-1, keepdims=True) + RMS_EPS) * w_ref[...] * (1.0 - lam_init))
    o_ref[...] = jnp.concatenate(outs, axis=1).astype(o_ref.dtype)


def attn_prefill(q, k, v, lam_vec, subln_w, *, batch, seq, tq, lam_init, hpb=2):
    n, hd = q.shape
    heads = hd // LANES
    nq = seq // tq
    kv_spec = pl.BlockSpec((seq, hpb * LANES), lambda b, h, i: (b, h))
    q_spec = pl.BlockSpec((tq, hpb * LANES), lambda b, h, i: (b * nq + i, h))
    return pl.pallas_call(
        functools.partial(_attn_prefill_kernel, tq=tq, lam_init=lam_init, hpb=hpb),
        out_shape=jax.ShapeDtypeStruct((n, hd), BF16),
        grid=(batch, heads // hpb, nq),
        in_specs=[pl.BlockSpec(lam_vec.shape, lambda b, h, i: (0, 0)), q_spec, kv_spec, kv_spec,
                  pl.BlockSpec((1, LANES), lambda b, h, i: (0, 0))],
        out_specs=q_spec,
        compiler_params=_params("parallel", "parallel", "arbitrary"),
        name="attn_prefill",
    )(lam_vec, q, k, v, subln_w.reshape(1, LANES))


def _attn_decode_kernel(pt_ref, lam_ref, q_ref, kn_ref, vn_ref, *rest, lam_init, pg):
    kc_refs, vc_refs = rest[:pg], rest[pg:2 * pg]
    w_ref, o_ref, m_ref, l_ref, acc_ref = rest[2 * pg:]
    p = pl.program_id(1)
    q = q_ref[0]
    heads = q.shape[0]
    lane = lax.broadcasted_iota(I32, (1, LANES), 1)
    lo = lane < DIFF_HEAD_DIM
    q2 = jnp.concatenate([jnp.where(lo, q, 0.0), jnp.where(lo, 0.0, q)], axis=0)

    @pl.when(p == 0)
    def _():
        kn2 = jnp.concatenate([kn_ref[0], kn_ref[0]], axis=0)
        m_ref[...] = jnp.sum(q2 * kn2, axis=-1, keepdims=True)
        l_ref[...] = jnp.ones_like(l_ref)
        acc_ref[...] = jnp.concatenate([vn_ref[0], vn_ref[0]], axis=0)

    n_rows = kc_refs[0].shape[1] * heads
    own = (lax.broadcasted_iota(I32, (2 * heads, n_rows), 1) % heads
           == lax.broadcasted_iota(I32, (2 * heads, n_rows), 0) % heads)
    q16 = q2.astype(BF16)
    s = [jnp.where(own, _dot_nt(q16, kc_refs[j][0].reshape(n_rows, LANES).astype(BF16)), NEG_BIG)
         for j in range(pg)]
    m_old = m_ref[...]
    m_new = m_old
    for j in range(pg):
        m_new = jnp.maximum(m_new, jnp.max(s[j], axis=1, keepdims=True))
    alpha = jnp.exp(m_old - m_new)
    l_new = alpha * l_ref[...]
    acc = alpha * acc_ref[...]
    for j in range(pg):
        pr = jnp.exp(s[j] - m_new)
        l_new = l_new + jnp.sum(pr, axis=1, keepdims=True)
        acc = acc + _dot(pr.astype(BF16), vc_refs[j][0].reshape(n_rows, LANES).astype(BF16))
    l_ref[...] = l_new
    acc_ref[...] = acc
    m_ref[...] = m_new

    @pl.when(p == pl.num_programs(1) - 1)
    def _():
        lam = _lambda_value(lam_ref, lam_init)
        o = (acc_ref[0:heads, :] / l_ref[0:heads, :]
             - lam * (acc_ref[heads:2 * heads, :] / l_ref[heads:2 * heads, :]))
        o = o * lax.rsqrt(jnp.mean(o * o, axis=-1, keepdims=True) + RMS_EPS) * w_ref[...] * (1.0 - lam_init)
        o_ref[0] = o.astype(o_ref.dtype)


def attn_decode(q, k_new, v_new, cache_k, cache_v, page_table, lam_vec, subln_w, *, lam_init):
    b, heads, _ = q.shape
    n_pages = page_table.shape[1]
    page = cache_k.shape[1]
    pg = math.gcd(n_pages, DECODE_PAGES_PER_STEP)
    tok = pl.BlockSpec((1, heads, LANES), lambda i, p, pt: (i, 0, 0))

    def cache(j):
        return pl.BlockSpec((1, page, heads, LANES), lambda i, p, pt: (pt[i * n_pages + p * pg + j], 0, 0, 0))

    grid_spec = pltpu.PrefetchScalarGridSpec(
        num_scalar_prefetch=1,
        grid=(b, n_pages // pg),
        in_specs=[pl.BlockSpec(lam_vec.shape, lambda i, p, pt: (0, 0)), tok, tok, tok]
        + [cache(j) for j in range(pg)] * 2
        + [pl.BlockSpec((1, LANES), lambda i, p, pt: (0, 0))],
        out_specs=tok,
        scratch_shapes=[pltpu.VMEM((2 * heads, 1), F32), pltpu.VMEM((2 * heads, 1), F32),
                        pltpu.VMEM((2 * heads, LANES), F32)],
    )
    return pl.pallas_call(
        functools.partial(_attn_decode_kernel, lam_init=lam_init, pg=pg),
        out_shape=jax.ShapeDtypeStruct(q.shape, BF16),
        grid_spec=grid_spec,
        compiler_params=_params("parallel", "arbitrary"),
        name="attn_decode",
    )(page_table.reshape(-1), lam_vec, q, k_new, v_new, *([cache_k] * pg), *([cache_v] * pg),
      subln_w.reshape(1, LANES))


def _router_kernel(x_ref, sc_ref, sh_ref, wr_ref, br_ref, idx_ref, gate_ref, pos_ref, cnt_ref,
                   tri_ref, carry_ref, *, tm):
    i = pl.program_id(0)

    @pl.when(i == 0)
    def _():
        r = lax.broadcasted_iota(I32, (tm, tm), 0)
        c = lax.broadcasted_iota(I32, (tm, tm), 1)
        tri_ref[...] = jnp.where(c < r, 1.0, 0.0).astype(BF16)
        carry_ref[...] = jnp.zeros_like(carry_ref)

    h = x_ref[...] * (1.0 + sc_ref[0]) + sh_ref[0]
    h_hi = h.astype(BF16)
    h_lo = (h - h_hi.astype(F32)).astype(BF16)
    w = wr_ref[...]
    w_hi = w.astype(BF16)
    w_lo = (w - w_hi.astype(F32)).astype(BF16)
    logits = _dot(h_hi, w_hi) + _dot(h_lo, w_hi) + _dot(h_hi, w_lo) + br_ref[...]
    lane = lax.broadcasted_iota(I32, (1, LANES), 1)
    work = jnp.where(lane < N_EXPERTS, logits, NEG_BIG)
    tops, idxs = [], []
    member = jnp.zeros((tm, LANES), F32)
    for _ in range(TOP_K):
        mx = jnp.max(work, axis=1, keepdims=True)
        ix = jnp.min(jnp.where(work == mx, lane, LANES), axis=1, keepdims=True)
        hit = lane == ix
        member = jnp.where(hit, 1.0, member)
        work = jnp.where(hit, NEG_BIG, work)
        tops.append(mx)
        idxs.append(ix)
    ex = [jnp.exp(t - tops[0]) for t in tops]
    denom = ex[0] + ex[1] + ex[2] + ex[3]
    rank = _dot(tri_ref[...], member.astype(BF16)) + carry_ref[...]
    idx_out = jnp.zeros((tm, LANES), I32)
    gate_out = jnp.zeros((tm, LANES), F32)
    pos_out = jnp.zeros((tm, LANES), I32)
    for k in range(TOP_K):
        pk = jnp.sum(jnp.where(lane == idxs[k], rank, 0.0), axis=1, keepdims=True)
        idx_out = jnp.where(lane == k, idxs[k], idx_out)
        gate_out = jnp.where(lane == k, ex[k] / denom, gate_out)
        pos_out = jnp.where(lane == k, pk.astype(I32), pos_out)
    idx_ref[...] = idx_out
    gate_ref[...] = gate_out
    pos_ref[...] = pos_out
    carry_ref[...] = carry_ref[...] + jnp.sum(member, axis=0, keepdims=True)
    cnt_ref[...] = carry_ref[...].astype(I32)


def router(x, sc, sh, w_r, b_r, *, rows_per_group, tm):
    n, d = x.shape
    tiles_per_group = rows_per_group // tm
    r = sc.shape[1]
    w_pad = jnp.zeros((d, LANES), F32).at[:, :N_EXPERTS].set(w_r)
    b_pad = jnp.zeros((1, LANES), F32).at[0, :N_EXPERTS].set(b_r)
    mod = pl.BlockSpec((1, r, d), lambda i: (i // tiles_per_group, 0, 0))
    wide = pl.BlockSpec((tm, LANES), lambda i: (i, 0))
    return pl.pallas_call(
        functools.partial(_router_kernel, tm=tm),
        out_shape=(jax.ShapeDtypeStruct((n, LANES), I32), jax.ShapeDtypeStruct((n, LANES), F32),
                   jax.ShapeDtypeStruct((n, LANES), I32), jax.ShapeDtypeStruct((1, LANES), I32)),
        grid=(n // tm,),
        in_specs=[pl.BlockSpec((tm, d), lambda i: (i, 0)), mod, mod,
                  pl.BlockSpec((d, LANES), lambda i: (0, 0)), pl.BlockSpec((1, LANES), lambda i: (0, 0))],
        out_specs=(wide, wide, wide, pl.BlockSpec((1, LANES), lambda i: (0, 0))),
        scratch_shapes=[pltpu.VMEM((tm, tm), BF16), pltpu.VMEM((1, LANES), F32)],
        compiler_params=_params("arbitrary"),
        name="moe_router",
    )(x, sc, sh, w_pad, b_pad)


def _dispatch_kernel(gap_lo_ref, gap_hi_ref, x_ref, sc_ref, sh_ref, dest_ref, xs_ref,
                     h_ref, zero_ref, idx_ref, sem_ref, isem_ref, *, tm, n_gap_rows):
    i = pl.program_id(0)
    n_idx = tm * TOP_K
    idx_copy = pltpu.make_async_copy(dest_ref.at[0, 0], idx_ref, isem_ref)
    idx_copy.start()
    h = x_ref[...] * (1.0 + sc_ref[0]) + sh_ref[0]
    lg = h.shape[1] // LANES
    for g in range(lg):
        h_ref[pl.ds(g, tm, stride=lg), :] = h[:, g * LANES:(g + 1) * LANES]
    idx_copy.wait()

    def row_copy(src, src_tok, dst_tok):
        return pltpu.make_async_copy(_token_rows(src, src_tok, lg), _token_rows(xs_ref, dst_tok, lg), sem_ref)

    def issue(r, carry):
        for k in range(TOP_K):
            row_copy(h_ref, r, idx_ref[r * TOP_K + k]).start()
        return carry

    lax.fori_loop(0, tm, issue, 0, unroll=DMA_UNROLL)

    def drain(r, carry):
        row_copy(h_ref, 0, 0).wait()
        return carry

    lax.fori_loop(0, n_idx, drain, 0, unroll=DMA_UNROLL)

    @pl.when(i == 0)
    def _():
        zero_ref[...] = jnp.zeros_like(zero_ref)

        def per_expert(e, carry):
            def fill(rw, c2):
                row_copy(zero_ref, 0, rw).start()
                return c2
            return lax.fori_loop(gap_lo_ref[e], gap_hi_ref[e], fill, carry)

        lax.fori_loop(0, N_EXPERTS, per_expert, 0)

        def drain_zero(r, carry):
            row_copy(zero_ref, 0, 0).wait()
            return carry

        lax.fori_loop(0, n_gap_rows, drain_zero, 0)


def dispatch(x, sc, sh, dest, gap_lo, gap_hi, *, rows_per_group, tm, n_slots_padded):
    n, d = x.shape
    tiles_per_group = rows_per_group // tm
    r = sc.shape[1]
    n_idx = tm * TOP_K
    mod = pl.BlockSpec((1, r, d), lambda i, lo, hi: (i // tiles_per_group, 0, 0))
    grid_spec = pltpu.PrefetchScalarGridSpec(
        num_scalar_prefetch=2,
        grid=(n // tm,),
        in_specs=[pl.BlockSpec((tm, d), lambda i, lo, hi: (i, 0)), mod, mod,
                  pl.BlockSpec((1, 1, n_idx), lambda i, lo, hi: (i, 0, 0))],
        out_specs=pl.BlockSpec(memory_space=pl.ANY),
        scratch_shapes=[pltpu.VMEM((tm * (d // LANES), LANES), F32), pltpu.VMEM((d // LANES, LANES), F32),
                        pltpu.SMEM((n_idx,), I32), pltpu.SemaphoreType.DMA, pltpu.SemaphoreType.DMA],
    )
    return pl.pallas_call(
        functools.partial(_dispatch_kernel, tm=tm, n_gap_rows=n_slots_padded - n * TOP_K),
        out_shape=jax.ShapeDtypeStruct((n_slots_padded * (d // LANES), LANES), F32),
        grid_spec=grid_spec,
        compiler_params=_params("arbitrary"),
        name="moe_dispatch",
    )(gap_lo, gap_hi, x, sc, sh, dest.reshape(n // tm, 1, n_idx))


def _prep_expert_kernel(wgu_ref, wd_ref, wgu_out, wd_out):
    half = GU_GROUP // 2
    r = lax.broadcasted_iota(I32, (GU_GROUP, GU_GROUP), 0)
    c = lax.broadcasted_iota(I32, (GU_GROUP, GU_GROUP), 1)
    perm = jnp.where(r == jnp.where(c < half, 2 * c, 2 * (c - half) + 1), 1.0, 0.0).astype(BF16)
    for g in range(wgu_ref.shape[2] // GU_GROUP):
        cols = slice(g * GU_GROUP, (g + 1) * GU_GROUP)
        wgu_out[0, :, cols] = _dot(wgu_ref[0, :, cols].astype(BF16), perm).astype(BF16)
    wd_out[0] = wd_ref[0].astype(BF16)


def prep_expert_weights(w_gu, w_down):
    ne, d, de2 = w_gu.shape
    gu_spec = pl.BlockSpec((1, d, de2), lambda e: (e, 0, 0))
    dn_spec = pl.BlockSpec((1, de2 // 2, d), lambda e: (e, 0, 0))
    return pl.pallas_call(
        _prep_expert_kernel,
        out_shape=(jax.ShapeDtypeStruct(w_gu.shape, BF16), jax.ShapeDtypeStruct(w_down.shape, BF16)),
        grid=(ne,),
        in_specs=[gu_spec, dn_spec],
        out_specs=(gu_spec, dn_spec),
        compiler_params=_params("parallel"),
        name="moe_prep_weights",
    )(w_gu, w_down)


def _expert_ffn_kernel(be_ref, xs_ref, wgu_ref, bgu_ref, wd_ref, bd_ref, ys_ref, *, blk):
    lg = xs_ref.shape[0] // blk
    x = jnp.concatenate([xs_ref[pl.ds(g, blk, stride=lg), :] for g in range(lg)], axis=1).astype(BF16)
    gu = _dot(x, wgu_ref[0]) + bgu_ref[0]
    half = GU_GROUP // 2
    acts = []
    for g in range(gu.shape[1] // GU_GROUP):
        gate = jnp.minimum(gu[:, g * GU_GROUP:g * GU_GROUP + half], SWIGLU_LIMIT)
        up = jnp.clip(gu[:, g * GU_GROUP + half:(g + 1) * GU_GROUP], -SWIGLU_LIMIT, SWIGLU_LIMIT)
        acts.append((up + 1.0) * (gate * _sigmoid(gate * SWIGLU_ALPHA)))
    act = jnp.concatenate(acts, axis=1).astype(BF16)
    y = _dot(act, wd_ref[0]) + bd_ref[0]
    for g in range(lg):
        ys_ref[pl.ds(g, blk, stride=lg), :] = y[:, g * LANES:(g + 1) * LANES]


def expert_ffn(xs, block_expert, wgu, bgu, wd, bd, *, blk, expert_base):
    d, de2 = wgu.shape[1], wgu.shape[2]
    lg = d // LANES
    ns = xs.shape[0] // lg
    rows = pl.BlockSpec((blk * lg, LANES), lambda i, be: (i, 0))
    wspec = lambda a, b: pl.BlockSpec((1, a, b), lambda i, be: (expert_base + be[i], 0, 0))
    grid_spec = pltpu.PrefetchScalarGridSpec(
        num_scalar_prefetch=1,
        grid=(ns // blk,),
        in_specs=[rows, wspec(d, de2), wspec(1, de2), wspec(de2 // 2, d), wspec(1, d)],
        out_specs=rows,
    )
    return pl.pallas_call(
        functools.partial(_expert_ffn_kernel, blk=blk),
        out_shape=jax.ShapeDtypeStruct(xs.shape, F32),
        grid_spec=grid_spec,
        compiler_params=_params("arbitrary"),
        name="moe_expert_ffn",
    )(block_expert, xs, wgu, bgu, wd, bd)


def _combine_ln_kernel(x_ref, gm_ref, gate_ref, dest_ref, ys_ref, lg_ref, lb_ref, o_ref,
                       buf_ref, idx_ref, sem_ref, isem_ref, *, tm, alpha):
    n_idx = tm * TOP_K
    idx_copy = pltpu.make_async_copy(dest_ref.at[0, 0], idx_ref, isem_ref)
    idx_copy.start()
    idx_copy.wait()

    lg = buf_ref.shape[1] // tm

    def row_copy(src_tok, k, r):
        return pltpu.make_async_copy(_token_rows(ys_ref, src_tok, lg), _token_rows(buf_ref.at[k], r, lg), sem_ref)

    def issue(r, carry):
        for k in range(TOP_K):
            row_copy(idx_ref[r * TOP_K + k], k, r).start()
        return carry

    lax.fori_loop(0, tm, issue, 0, unroll=DMA_UNROLL)

    def drain(r, carry):
        row_copy(0, 0, 0).wait()
        return carry

    lax.fori_loop(0, n_idx, drain, 0, unroll=DMA_UNROLL)
    gates = gate_ref[...]
    cols = []
    for g in range(lg):
        yg = gates[:, 0:1] * buf_ref[0, pl.ds(g, tm, stride=lg), :]
        for k in range(1, TOP_K):
            yg = yg + gates[:, k:k + 1] * buf_ref[k, pl.ds(g, tm, stride=lg), :]
        cols.append(yg)
    y = jnp.concatenate(cols, axis=1)
    o_ref[...] = _layer_norm(alpha * x_ref[...] + gm_ref[0] * y, lg_ref[...], lb_ref[...])


def combine_ln(x, gm, gates, dest, ys, ln_g, ln_b, *, rows_per_group, tm, alpha):
    n, d = x.shape
    tiles_per_group = rows_per_group // tm
    r = gm.shape[1]
    n_idx = tm * TOP_K
    row = pl.BlockSpec((tm, d), lambda i: (i, 0))
    vec = pl.BlockSpec((1, d), lambda i: (0, 0))
    return pl.pallas_call(
        functools.partial(_combine_ln_kernel, tm=tm, alpha=alpha),
        out_shape=jax.ShapeDtypeStruct((n, d), F32),
        grid=(n // tm,),
        in_specs=[row, pl.BlockSpec((1, r, d), lambda i: (i // tiles_per_group, 0, 0)),
                  pl.BlockSpec((tm, LANES), lambda i: (i, 0)),
                  pl.BlockSpec((1, 1, n_idx), lambda i: (i, 0, 0)),
                  pl.BlockSpec(memory_space=pl.ANY), vec, vec],
        out_specs=row,
        scratch_shapes=[pltpu.VMEM((TOP_K, tm * (d // LANES), LANES), F32), pltpu.SMEM((n_idx,), I32),
                        pltpu.SemaphoreType.DMA, pltpu.SemaphoreType.DMA],
        compiler_params=_params("arbitrary"),
        name="moe_combine_ln",
    )(x, gm, gates, dest.reshape(n // tm, 1, n_idx), ys, ln_g.reshape(1, d), ln_b.reshape(1, d))


def _moe_layer(x, sc, sh, gm, ln_g, ln_b, w_r, b_r, wgu, bgu, wd, bd, *, expert_base, rows_per_group, tm, blk,
               alpha):
    n, d = x.shape
    idx_w, gate_w, pos_w, cnt_w = router(x, sc, sh, w_r, b_r, rows_per_group=rows_per_group, tm=tm)
    counts = cnt_w[0, :N_EXPERTS]
    top_idx, pos = idx_w[:, :TOP_K], pos_w[:, :TOP_K]
    n_slots = n * TOP_K
    n_blocks = -(-(n_slots + N_EXPERTS * (blk - 1)) // blk)
    padded = (counts + blk - 1) // blk * blk
    pad_end = jnp.cumsum(padded)
    pad_start = pad_end - padded
    onehot = top_idx[..., None] == jnp.arange(N_EXPERTS, dtype=I32)
    dest = (jnp.sum(jnp.where(onehot, pad_start, 0), axis=-1) + pos).astype(I32)
    block_start = jnp.arange(n_blocks, dtype=I32) * blk
    block_expert = jnp.minimum(
        jnp.sum((pad_end[None, :] <= block_start[:, None]).astype(I32), axis=1), N_EXPERTS - 1).astype(I32)
    gap_lo = (pad_start + counts).astype(I32)
    gap_hi = jnp.concatenate([pad_start[1:], jnp.full((1,), n_blocks * blk, I32)]).astype(I32)
    xs = dispatch(x, sc, sh, dest, gap_lo, gap_hi, rows_per_group=rows_per_group, tm=tm,
                  n_slots_padded=n_blocks * blk)
    ys = expert_ffn(xs, block_expert, wgu, bgu, wd, bd, blk=blk, expert_base=expert_base)
    return combine_ln(x, gm, gate_w, dest, ys, ln_g, ln_b, rows_per_group=rows_per_group, tm=tm, alpha=alpha)


def _rope_tables(pos):
    inv_freq = ROPE_THETA ** (-jnp.arange(ROPE_HALF, dtype=F32) / ROPE_HALF)
    ang = pos.astype(F32)[:, None] * inv_freq[None, :]
    cos, sin = jnp.cos(ang), jnp.sin(ang)
    reps = LANES // DIFF_HEAD_DIM
    return (jnp.concatenate([cos, cos] * reps, axis=1), jnp.concatenate([-sin, sin] * reps, axis=1))


def _trunk(x, mods, kv_mods, past_len, s0, cache, wts, *, batch, seq, tm, blk):
    n, d = x.shape
    depth = wts["ada_w"].shape[0]
    n_a = wts["hgrn_w_in"].shape[0]
    alpha = (2 * depth) ** 0.25
    rows_per_group = seq if seq >= tm else n
    kw = dict(rows_per_group=rows_per_group, tm=tm)
    decode = seq == 1
    pos = past_len + (jnp.zeros((n,), I32) if decode else jnp.arange(seq, dtype=I32))
    rope = _rope_tables(pos)
    heads = d // LANES
    states = []
    k_new = v_new = k16 = v16 = None
    for l in range(depth):
        sh_m, sc_m, g_m, sh_f, sc_f, g_f = mods[l]
        if l < n_a:
            proj, = modmm(x, sc_m, sh_m, wts["hgrn_w_in"][l], out_dtypes=(F32,), **kw)
            lbs = [a[l][None] for a in wts["lb"]]
            nw = wts["hgrn_norm_w"][l].reshape(1, LANES)
            if decode:
                o, s_l = hgrn_step(proj, *lbs, nw, s0[l])
            else:
                o, s_l = hgrn_chunked(proj, *lbs, nw, s0[l], batch=batch, seq=seq, tb=min(seq, 512))
            states.append(s_l)
            w_out = wts["hgrn_w_out"][l]
        else:
            j = l - n_a
            lam_init = 0.8 - 0.6 * math.exp(-0.3 * l)
            if l == n_a:
                kv_shift, kv_scale = kv_mods
                half = wts["kv_w"].shape[1] // 2
                k_new, k16 = modmm(x, kv_scale, kv_shift, wts["kv_w"][:, :half], out_dtypes=(F32, BF16),
                                   rope=rope, **kw)
                v_new, v16 = modmm(x, kv_scale, kv_shift, wts["kv_w"][:, half:], out_dtypes=(F32, BF16), **kw)
            q_scale = DIFF_HEAD_DIM ** -0.5
            if decode:
                q, = modmm(x, sc_m, sh_m, wts["diff_w_q"][j], out_dtypes=(F32,), rope=rope, out_scale=q_scale, **kw)
                o = attn_decode(q.reshape(n, heads, LANES), k_new.reshape(n, heads, LANES),
                                v_new.reshape(n, heads, LANES), cache[0], cache[1], cache[2],
                                wts["diff_lambda"][j], wts["diff_subln_w"][j], lam_init=lam_init)
                o = o.reshape(n, d)
            else:
                q, = modmm(x, sc_m, sh_m, wts["diff_w_q"][j], out_dtypes=(BF16,), rope=rope, out_scale=q_scale, **kw)
                o = attn_prefill(q, k16, v16, wts["diff_lambda"][j], wts["diff_subln_w"][j],
                                 batch=batch, seq=seq, tq=min(seq, 1024), lam_init=lam_init, hpb=1)
            w_out = wts["diff_w_out"][j]
        x = proj_ln(o, w_out, x, g_m, wts["ln_g"][l, 0], wts["ln_b"][l, 0], alpha=alpha, **kw)
        x = _moe_layer(x, sc_f, sh_f, g_f, wts["ln_g"][l, 1], wts["ln_b"][l, 1],
                       wts["moe_w_router"][l], wts["moe_b_router"][l],
                       wts["moe_w_gu"], wts["moe_b_gu"], wts["moe_w_down"], wts["moe_b_down"],
                       expert_base=l * N_EXPERTS, alpha=alpha, blk=blk, **kw)
    return x, k_new, v_new, jnp.stack(states)


def kernel(x_prompt, x_sample, cache_k, cache_v, state_hgrn, page_table, c_prompt, c_sample, ada_w, ada_b, ln_g, ln_b, hgrn_w_in, hgrn_lb_logits, hgrn_norm_w, hgrn_w_out, kv_ada_w, kv_ada_b, kv_w, diff_w_q, diff_lambda, diff_subln_w, diff_w_out, moe_w_router, moe_b_router, moe_w_gu, moe_b_gu, moe_w_down, moe_b_down):
    b_p, t_p, d = x_prompt.shape
    b_d, t_d, _ = x_sample.shape
    depth = ada_w.shape[0]
    n_a = hgrn_w_in.shape[0]
    heads = d // LANES
    de = moe_w_down.shape[2]

    m_rows = -(-(b_d + b_p) // 8) * 8
    c_all = jnp.zeros((m_rows, d), F32).at[:b_d].set(c_sample).at[b_d:b_d + b_p].set(c_prompt)
    mod_all = adaln(c_all, ada_w, ada_b)
    kv_mod_all = adaln(c_all, kv_ada_w[None], kv_ada_b[None])[0]

    def split_mods(m, n_vec, lo, hi, per_row):
        parts = [m[lo:hi, v * d:(v + 1) * d] for v in range(n_vec)]
        return [p[None] if per_row else p[:, None, :] for p in parts]

    mods_p = [split_mods(mod_all[l], 6, b_d, b_d + b_p, False) for l in range(depth)]
    mods_s = [split_mods(mod_all[l], 6, 0, b_d, True) for l in range(depth)]
    kv_mods_p = split_mods(kv_mod_all, 2, b_d, b_d + b_p, False)
    kv_mods_s = split_mods(kv_mod_all, 2, 0, b_d, True)

    loglb, log1m, onem = hgrn_lower_bounds(hgrn_lb_logits)

    n_exp = moe_w_gu.shape[1]
    w_gu16, w_down16 = prep_expert_weights(moe_w_gu.reshape(depth * n_exp, d, 2 * de),
                                           moe_w_down.reshape(depth * n_exp, de, d))
    half = GU_GROUP // 2
    b_gu = moe_b_gu.reshape(depth, n_exp, 2 * de // GU_GROUP, half, 2)
    b_gu = jnp.swapaxes(b_gu, -1, -2).reshape(depth * n_exp, 1, 2 * de)

    wts = dict(
        ada_w=ada_w, ln_g=ln_g, ln_b=ln_b,
        hgrn_w_in=hgrn_w_in.astype(BF16), lb=(loglb, log1m, onem), hgrn_norm_w=hgrn_norm_w,
        hgrn_w_out=hgrn_w_out.astype(BF16), kv_w=kv_w.astype(BF16), diff_w_q=diff_w_q.astype(BF16),
        diff_lambda=diff_lambda, diff_subln_w=diff_subln_w, diff_w_out=diff_w_out.astype(BF16),
        moe_w_router=moe_w_router, moe_b_router=moe_b_router,
        moe_w_gu=w_gu16, moe_b_gu=b_gu, moe_w_down=w_down16, moe_b_down=moe_b_down.reshape(depth * n_exp, 1, d),
    )

    s0_prompt = jnp.zeros((n_a, b_p, heads, LANES, LANES), F32)
    y_p, k_p, v_p, st_p = _trunk(x_prompt.reshape(b_p * t_p, d), mods_p, kv_mods_p, 0, s0_prompt, None, wts,
                                 batch=b_p, seq=t_p, tm=min(512, t_p), blk=256)
    past_len = page_table.shape[1] * cache_k.shape[1]
    y_s, k_s, v_s, st_s = _trunk(x_sample.reshape(b_d * t_d, d), mods_s, kv_mods_s, past_len, state_hgrn,
                                 (cache_k, cache_v, page_table), wts, batch=b_d, seq=t_d, tm=b_d, blk=16)
    dv = d // heads
    return (y_p.reshape(b_p, t_p, d), y_s.reshape(b_d, t_d, d),
            k_p.reshape(b_p, t_p, heads, dv), v_p.reshape(b_p, t_p, heads, dv), st_p,
            k_s.reshape(b_d, t_d, heads, dv), v_s.reshape(b_d, t_d, heads, dv), st_s)
```

```python
import functools
import math

import jax
import jax.numpy as jnp
from jax import lax
from jax.experimental import pallas as pl
from jax.experimental.pallas import tpu as pltpu

F32 = jnp.float32
BF16 = jnp.bfloat16
I32 = jnp.int32

LANES = 128
HGRN_CHUNK = 64
HGRN_SUB = 16
LB_FLOOR = 1e-30
LOG2E = math.log2(math.e)
DIFF_HEAD_DIM = 64
ROPE_THETA = 10000.0
ROPE_HALF = DIFF_HEAD_DIM // 2
NEG_BIG = -1e30
N_EXPERTS = 32
TOP_K = 4
SWIGLU_LIMIT = 7.0
SWIGLU_ALPHA = 1.702
LN_EPS = 1e-5
RMS_EPS = 1e-5
VMEM_LIMIT = 56 * 1024 * 1024
DECODE_PAGES_PER_STEP = 8
GU_GROUP = 256
DMA_UNROLL = 8


def _params(*sem):
    return pltpu.CompilerParams(dimension_semantics=sem, vmem_limit_bytes=VMEM_LIMIT)


def _sigmoid(x):
    return 1.0 / (1.0 + jnp.exp(-x))


def _silu(x):
    return x * _sigmoid(x)


def _log_sigmoid(x):
    return jnp.minimum(x, 0.0) - jnp.log1p(jnp.exp(-jnp.abs(x)))


def _layer_norm(z, g, b):
    mu = jnp.mean(z, axis=-1, keepdims=True)
    zc = z - mu
    var = jnp.mean(zc * zc, axis=-1, keepdims=True)
    return zc * lax.rsqrt(var + LN_EPS) * g + b


def _dot(a, b):
    return jnp.dot(a, b, preferred_element_type=F32)


def _dot_nt(a, b):
    return lax.dot_general(a, b, (((1,), (1,)), ((), ())), preferred_element_type=F32)


def _dot_tn(a, b):
    return lax.dot_general(a, b, (((0,), (0,)), ((), ())), preferred_element_type=F32)


def _token_rows(ref, tok, lg):
    start = tok * lg
    return ref.at[pl.ds(start if isinstance(start, int) else pl.multiple_of(start, lg), lg)]


def _split3(x):
    hi = x.astype(BF16)
    r1 = x - hi.astype(F32)
    mid = r1.astype(BF16)
    lo = (r1 - mid.astype(F32)).astype(BF16)
    return hi, mid, lo


def _adaln_kernel(c_ref, w_ref, b_ref, o_ref):
    s = _silu(c_ref[...]).astype(BF16)
    o_ref[0] = _dot(s, w_ref[0].astype(BF16)) + b_ref[0]


def adaln(c, w, b, tn=1024):
    m, d = c.shape
    nl, _, no = w.shape
    return pl.pallas_call(
        _adaln_kernel,
        out_shape=jax.ShapeDtypeStruct((nl, m, no), F32),
        grid=(nl, no // tn),
        in_specs=[
            pl.BlockSpec((m, d), lambda l, j: (0, 0)),
            pl.BlockSpec((1, d, tn), lambda l, j: (l, 0, j)),
            pl.BlockSpec((1, 1, tn), lambda l, j: (l, 0, j)),
        ],
        out_specs=pl.BlockSpec((1, m, tn), lambda l, j: (l, 0, j)),
        compiler_params=_params("parallel", "parallel"),
        name="adaln",
    )(c, w, b.reshape(nl, 1, no))


def _lb_kernel(logit_ref, loglb_ref, log1m_ref, onem_ref):
    x = logit_ref[...]
    nl = x.shape[0]
    e = jnp.exp(x - jnp.max(x, axis=0, keepdims=True))
    sm = e / jnp.sum(e, axis=0, keepdims=True)
    acc = jnp.zeros_like(sm[0:1])
    for l in range(nl):
        lb = jnp.clip(acc, 0.0, 1.0 - 1e-6)
        loglb_ref[l] = jnp.log(jnp.maximum(lb, LB_FLOOR))
        log1m_ref[l] = jnp.log1p(-lb)
        onem_ref[l] = 1.0 - lb
        acc = acc + sm[l:l + 1]


def hgrn_lower_bounds(lb_logits):
    nl, f = lb_logits.shape
    shp = jax.ShapeDtypeStruct((nl, 1, f), F32)
    return pl.pallas_call(_lb_kernel, out_shape=(shp, shp, shp), name="hgrn_lb")(lb_logits)


def _rotary_tile(y, cos, sin_signed):
    lane = lax.broadcasted_iota(I32, (1, LANES), 1)
    first_half = (lane % DIFF_HEAD_DIM) < ROPE_HALF
    partner = jnp.where(first_half, pltpu.roll(y, LANES - ROPE_HALF, 1), pltpu.roll(y, ROPE_HALF, 1))
    return y * cos + partner * sin_signed


def _modmm_kernel(*refs, rotary, out_scale, n_out):
    if rotary:
        x_ref, sc_ref, sh_ref, w_ref, cos_ref, sin_ref = refs[:6]
        rest = refs[6:]
    else:
        x_ref, sc_ref, sh_ref, w_ref = refs[:4]
        rest = refs[4:]
    out_refs, h_ref = rest[:n_out], rest[n_out]

    @pl.when(pl.program_id(1) == 0)
    def _():
        h_ref[...] = (x_ref[...] * (1.0 + sc_ref[0]) + sh_ref[0]).astype(BF16)

    y = _dot(h_ref[...], w_ref[...])
    if rotary:
        cos, sin = cos_ref[...], sin_ref[...]
        y = jnp.concatenate(
            [_rotary_tile(y[:, g * LANES:(g + 1) * LANES], cos, sin) for g in range(y.shape[1] // LANES)], axis=1)
    if out_scale != 1.0:
        y = y * out_scale
    for o_ref in out_refs:
        o_ref[...] = y.astype(o_ref.dtype)


def modmm(x, sc, sh, w, *, rows_per_group, tm, out_dtypes, rope=None, out_scale=1.0, tn=1024):
    n, d = x.shape
    no = w.shape[1]
    tiles_per_group = rows_per_group // tm
    r = sc.shape[1]
    in_specs = [
        pl.BlockSpec((tm, d), lambda i, j: (i, 0)),
        pl.BlockSpec((1, r, d), lambda i, j: (i // tiles_per_group, 0, 0)),
        pl.BlockSpec((1, r, d), lambda i, j: (i // tiles_per_group, 0, 0)),
        pl.BlockSpec((d, tn), lambda i, j: (0, j)),
    ]
    args = [x, sc, sh, w]
    if rope is not None:
        pos_tiles = rope[0].shape[0] // tm
        in_specs += [pl.BlockSpec((tm, LANES), lambda i, j: (i % pos_tiles, 0))] * 2
        args += list(rope)
    outs = tuple(jax.ShapeDtypeStruct((n, no), dt) for dt in out_dtypes)
    res = pl.pallas_call(
        functools.partial(_modmm_kernel, rotary=rope is not None, out_scale=out_scale, n_out=len(outs)),
        out_shape=outs,
        grid=(n // tm, no // tn),
        in_specs=in_specs,
        out_specs=tuple(pl.BlockSpec((tm, tn), lambda i, j: (i, j)) for _ in outs),
        scratch_shapes=[pltpu.VMEM((tm, d), BF16)],
        compiler_params=_params("parallel", "arbitrary"),
        name="modmm",
    )(*args)
    return res


def _hgrn_head_chunk(q_raw, f, v, g_raw, loglb, log1m, onem, nw, st, tri, sub_row):
    c, sub = HGRN_CHUNK, HGRN_SUB
    half = sub // 2
    q = _silu(q_raw)
    x2 = log1m + _log_sigmoid(f)
    mx = jnp.maximum(loglb, x2)
    logf = mx + jnp.log1p(jnp.exp(jnp.minimum(loglb, x2) - mx))
    kk = onem * _sigmoid(-f)
    hi, mid, lo = _split3(logf)
    b = (_dot(tri, hi) + _dot(tri, mid) + _dot(tri, lo)) * LOG2E
    v16 = v.astype(BF16)
    o_inter = _dot_nt((q * jnp.exp2(b)).astype(BF16), st.astype(BF16))
    o_parts = []
    for i in range(c // sub):
        sl = slice(i * sub, (i + 1) * sub)
        qi, bi, ki, vi = q[sl], b[sl], kk[sl], v[sl]
        oi = o_inter[sl]
        top, bot = oi[:half], oi[half:]
        for s in range(sub):
            k_s, b_s, v_s = ki[s:s + 1], bi[s:s + 1], vi[s:s + 1]
            a_bot = jnp.sum(qi[half:] * k_s * jnp.exp2(bi[half:] - b_s), axis=1, keepdims=True)
            if s < half:
                a_top = jnp.sum(qi[:half] * k_s * jnp.exp2(bi[:half] - b_s), axis=1, keepdims=True)
                top = top + jnp.where(sub_row >= s, a_top, 0.0) * v_s
            else:
                a_bot = jnp.where(sub_row >= s - half, a_bot, 0.0)
            bot = bot + a_bot * v_s
        oi = jnp.concatenate([top, bot], axis=0)
        if i > 0:
            r = b[i * sub - 1:i * sub]
            qs = (qi * jnp.exp2(bi - r)).astype(BF16)
            ks = (kk[:i * sub] * jnp.exp2(r - b[:i * sub])).astype(BF16)
            a = _dot_nt(qs, ks)
            oi = oi + _dot(a.astype(BF16), v16[:i * sub])
        o_parts.append(oi)
    o = jnp.concatenate(o_parts, axis=0)
    b_last = b[c - 1:c]
    kd = (kk * jnp.exp2(b_last - b)).astype(BF16)
    st_new = st * jnp.exp2(b_last) + _dot_tn(v16, kd)
    o = o * lax.rsqrt(jnp.mean(o * o, axis=-1, keepdims=True) + RMS_EPS) * nw * _silu(g_raw)
    return o, st_new


def _hgrn_chunk_kernel(q_ref, f_ref, v_ref, g_ref, loglb_ref, log1m_ref, onem_ref, nw_ref, s0_ref,
                       o_ref, sfin_ref, st_ref, *, n_chunks, hg):
    t = pl.program_id(2)
    c = HGRN_CHUNK

    @pl.when(t == 0)
    def _():
        for hh in range(hg):
            st_ref[hh] = s0_ref[0, hh].T

    loglb, log1m, onem, nw = loglb_ref[0], log1m_ref[0], onem_ref[0], nw_ref[...]
    row = lax.broadcasted_iota(I32, (c, c), 0)
    col = lax.broadcasted_iota(I32, (c, c), 1)
    tri = jnp.where(col <= row, 1.0, 0.0).astype(BF16)
    sub_row = lax.broadcasted_iota(I32, (HGRN_SUB // 2, 1), 0)

    def chunk(ci, carry):
        rows = pl.ds(pl.multiple_of(ci * c, c), c)
        outs = []
        for hh in range(hg):
            ln = slice(hh * LANES, (hh + 1) * LANES)
            o, st_new = _hgrn_head_chunk(q_ref[rows, ln], f_ref[rows, ln], v_ref[rows, ln], g_ref[rows, ln],
                                         loglb[:, ln], log1m[:, ln], onem[:, ln], nw, st_ref[hh], tri, sub_row)
            st_ref[hh] = st_new
            outs.append(o)
        o_ref[rows, :] = jnp.concatenate(outs, axis=1).astype(o_ref.dtype)
        return carry

    lax.fori_loop(0, n_chunks, chunk, 0)

    @pl.when(t == pl.num_programs(2) - 1)
    def _():
        for hh in range(hg):
            sfin_ref[0, hh] = st_ref[hh].T


def hgrn_chunked(proj, loglb, log1m, onem, norm_w, s0, *, batch, seq, tb, hg=8):
    n = proj.shape[0]
    heads = proj.shape[1] // (4 * LANES)
    tpb = seq // tb
    ngrp = heads // hg
    w = hg * LANES
    row_blk = lambda off: pl.BlockSpec((tb, w), lambda b, h, t: (b * tpb + t, off * ngrp + h))
    vec = pl.BlockSpec((1, 1, w), lambda b, h, t: (0, 0, h))
    st_spec = pl.BlockSpec((1, hg, LANES, LANES), lambda b, h, t: (b, h, 0, 0))
    return pl.pallas_call(
        functools.partial(_hgrn_chunk_kernel, n_chunks=tb // HGRN_CHUNK, hg=hg),
        out_shape=(jax.ShapeDtypeStruct((n, heads * LANES), BF16),
                   jax.ShapeDtypeStruct(s0.shape, F32)),
        grid=(batch, ngrp, tpb),
        in_specs=[row_blk(0), row_blk(1), row_blk(2), row_blk(3), vec, vec, vec,
                  pl.BlockSpec((1, LANES), lambda b, h, t: (0, 0)), st_spec],
        out_specs=(pl.BlockSpec((tb, w), lambda b, h, t: (b * tpb + t, h)), st_spec),
        scratch_shapes=[pltpu.VMEM((hg, LANES, LANES), F32)],
        compiler_params=_params("parallel", "parallel", "arbitrary"),
        name="hgrn_chunked",
    )(proj, proj, proj, proj, loglb, log1m, onem, norm_w, s0)


def _hgrn_step_kernel(q_ref, f_ref, v_ref, g_ref, loglb_ref, log1m_ref, onem_ref, nw_ref, s_ref,
                      o_ref, snew_ref, *, bb):
    f = f_ref[...]
    q = _silu(q_ref[...])
    v = v_ref[...]
    x1 = loglb_ref[0]
    x2 = log1m_ref[0] + _log_sigmoid(f)
    mx = jnp.maximum(x1, x2)
    decay = jnp.exp(mx + jnp.log1p(jnp.exp(jnp.minimum(x1, x2) - mx)))
    kk = onem_ref[0] * _sigmoid(-f)
    q_t, d_t, k_t = q.T, decay.T, kk.T
    rows = []
    for e in range(bb):
        s_new = s_ref[e, 0] * d_t[:, e:e + 1] + k_t[:, e:e + 1] * v[e:e + 1, :]
        snew_ref[e, 0] = s_new
        rows.append(jnp.sum(q_t[:, e:e + 1] * s_new, axis=0, keepdims=True))
    o = jnp.concatenate(rows, axis=0)
    o = o * lax.rsqrt(jnp.mean(o * o, axis=-1, keepdims=True) + RMS_EPS) * nw_ref[...] * _silu(g_ref[...])
    o_ref[...] = o.astype(o_ref.dtype)


def hgrn_step(proj, loglb, log1m, onem, norm_w, state, *, bb=16):
    n = proj.shape[0]
    heads = proj.shape[1] // (4 * LANES)
    row_blk = lambda off: pl.BlockSpec((bb, LANES), lambda i, h: (i, off * heads + h))
    vec = pl.BlockSpec((1, 1, LANES), lambda i, h: (0, 0, h))
    st_spec = pl.BlockSpec((bb, 1, LANES, LANES), lambda i, h: (i, h, 0, 0))
    return pl.pallas_call(
        functools.partial(_hgrn_step_kernel, bb=bb),
        out_shape=(jax.ShapeDtypeStruct((n, heads * LANES), BF16),
                   jax.ShapeDtypeStruct(state.shape, F32)),
        grid=(n // bb, heads),
        in_specs=[row_blk(0), row_blk(1), row_blk(2), row_blk(3), vec, vec, vec,
                  pl.BlockSpec((1, LANES), lambda i, h: (0, 0)), st_spec],
        out_specs=(pl.BlockSpec((bb, LANES), lambda i, h: (i, h)), st_spec),
        compiler_params=_params("parallel", "parallel"),
        name="hgrn_step",
    )(proj, proj, proj, proj, loglb, log1m, onem, norm_w, state)


def _proj_ln_kernel(a_ref, w_ref, x_ref, gm_ref, lg_ref, lb_ref, o_ref, *, alpha):
    y = _dot(a_ref[...], w_ref[...])
    o_ref[...] = _layer_norm(alpha * x_ref[...] + gm_ref[0] * y, lg_ref[...], lb_ref[...])


def proj_ln(a, w, x, gm, ln_g, ln_b, *, rows_per_group, tm, alpha):
    n, d = x.shape
    tiles_per_group = rows_per_group // tm
    r = gm.shape[1]
    row = pl.BlockSpec((tm, d), lambda i: (i, 0))
    vec = pl.BlockSpec((1, d), lambda i: (0, 0))
    return pl.pallas_call(
        functools.partial(_proj_ln_kernel, alpha=alpha),
        out_shape=jax.ShapeDtypeStruct((n, d), F32),
        grid=(n // tm,),
        in_specs=[row, pl.BlockSpec((d, d), lambda i: (0, 0)), row,
                  pl.BlockSpec((1, r, d), lambda i: (i // tiles_per_group, 0, 0)), vec, vec],
        out_specs=row,
        compiler_params=_params("parallel"),
        name="proj_ln",
    )(a, w, x, gm, ln_g.reshape(1, d), ln_b.reshape(1, d))


def _lambda_value(lam_ref, lam_init):
    lv = lam_ref[...]
    s1 = jnp.sum(lv[0:1] * lv[1:2], axis=1, keepdims=True)
    s2 = jnp.sum(lv[2:3] * lv[3:4], axis=1, keepdims=True)
    return jnp.exp(s1) - jnp.exp(s2) + lam_init


def _attn_prefill_kernel(lam_ref, q_ref, k_ref, v_ref, w_ref, o_ref, *, tq, lam_init, hpb):
    i = pl.program_id(2)
    lane = lax.broadcasted_iota(I32, (1, LANES), 1)
    keep = lax.broadcasted_iota(I32, (tq, tq), 1) <= lax.broadcasted_iota(I32, (tq, tq), 0)
    q_maps = []
    for hh in range(hpb):
        q = q_ref[:, hh * LANES:(hh + 1) * LANES]
        zero = jnp.zeros_like(q)
        q_maps += [jnp.where(lane < DIFF_HEAD_DIM, q, zero), jnp.where(lane < DIFF_HEAD_DIM, zero, q)]

    def update(s, vj, m, l, acc):
        m_new = jnp.maximum(m, jnp.max(s, axis=1, keepdims=True))
        alpha = jnp.exp(m - m_new)
        p = jnp.exp(s - m_new)
        l = alpha * l + jnp.sum(p, axis=1, keepdims=True)
        acc = alpha * acc + _dot(p.astype(BF16), vj)
        return m_new, l, acc

    def block(j, carry, masked):
        rows = pl.ds(pl.multiple_of(j * tq, tq), tq)
        out = ()
        for hh in range(hpb):
            ln = slice(hh * LANES, (hh + 1) * LANES)
            kj, vj = k_ref[rows, ln], v_ref[rows, ln]
            for c in range(2):
                s = _dot_nt(q_maps[2 * hh + c], kj)
                if masked:
                    s = jnp.where(keep, s, NEG_BIG)
                st = 3 * (2 * hh + c)
                out += update(s, vj, *carry[st:st + 3])
        return out

    m0 = jnp.full((tq, 1), NEG_BIG, F32)
    l0 = jnp.zeros((tq, 1), F32)
    a0 = jnp.zeros((tq, LANES), F32)
    carry = lax.fori_loop(0, i, lambda j, c: block(j, c, False), (m0, l0, a0) * (2 * hpb))
    fin = block(i, carry, True)
    lam = _lambda_value(lam_ref, lam_init)
    outs = []
    for hh in range(hpb):
        _, l1, a1, _, l2, a2 = fin[6 * hh:6 * hh + 6]
        o = a1 / l1 - lam * (a2 / l2)
        outs.append(o * lax.rsqrt(jnp.mean(o * o, axis=-1, keepdims=True) + RMS_EPS) * w_ref[...] * (1.0 - lam_init))
    o_ref[...] = jnp.concatenate(outs, axis=1).astype(o_ref.dtype)


def attn_prefill(q, k, v, lam_vec, subln_w, *, batch, seq, tq, lam_init, hpb=2):
    n, hd = q.shape
    heads = hd // LANES
    nq = seq // tq
    kv_spec = pl.BlockSpec((seq, hpb * LANES), lambda b, h, i: (b, h))
    q_spec = pl.BlockSpec((tq, hpb * LANES), lambda b, h, i: (b * nq + i, h))
    return pl.pallas_call(
        functools.partial(_attn_prefill_kernel, tq=tq, lam_init=lam_init, hpb=hpb),
        out_shape=jax.ShapeDtypeStruct((n, hd), BF16),
        grid=(batch, heads // hpb, nq),
        in_specs=[pl.BlockSpec(lam_vec.shape, lambda b, h, i: (0, 0)), q_spec, kv_spec, kv_spec,
                  pl.BlockSpec((1, LANES), lambda b, h, i: (0, 0))],
        out_specs=q_spec,
        compiler_params=_params("parallel", "parallel", "arbitrary"),
        name="attn_prefill",
    )(lam_vec, q, k, v, subln_w.reshape(1, LANES))


def _attn_decode_kernel(pt_ref, lam_ref, q_ref, kn_ref, vn_ref, *rest, lam_init, pg):
    kc_refs, vc_refs = rest[:pg], rest[pg:2 * pg]
    w_ref, o_ref, m_ref, l_ref, acc_ref = rest[2 * pg:]
    p = pl.program_id(1)
    q = q_ref[0]
    heads = q.shape[0]
    lane = lax.broadcasted_iota(I32, (1, LANES), 1)
    lo = lane < DIFF_HEAD_DIM
    q2 = jnp.concatenate([jnp.where(lo, q, 0.0), jnp.where(lo, 0.0, q)], axis=0)

    @pl.when(p == 0)
    def _():
        kn2 = jnp.concatenate([kn_ref[0], kn_ref[0]], axis=0)
        m_ref[...] = jnp.sum(q2 * kn2, axis=-1, keepdims=True)
        l_ref[...] = jnp.ones_like(l_ref)
        acc_ref[...] = jnp.concatenate([vn_ref[0], vn_ref[0]], axis=0)

    n_rows = kc_refs[0].shape[1] * heads
    own = (lax.broadcasted_iota(I32, (2 * heads, n_rows), 1) % heads
           == lax.broadcasted_iota(I32, (2 * heads, n_rows), 0) % heads)
    q16 = q2.astype(BF16)
    s = [jnp.where(own, _dot_nt(q16, kc_refs[j][0].reshape(n_rows, LANES).astype(BF16)), NEG_BIG)
         for j in range(pg)]
    m_old = m_ref[...]
    m_new = m_old
    for j in range(pg):
        m_new = jnp.maximum(m_new, jnp.max(s[j], axis=1, keepdims=True))
    alpha = jnp.exp(m_old - m_new)
    l_new = alpha * l_ref[...]
    acc = alpha * acc_ref[...]
    for j in range(pg):
        pr = jnp.exp(s[j] - m_new)
        l_new = l_new + jnp.sum(pr, axis=1, keepdims=True)
        acc = acc + _dot(pr.astype(BF16), vc_refs[j][0].reshape(n_rows, LANES).astype(BF16))
    l_ref[...] = l_new
    acc_ref[...] = acc
    m_ref[...] = m_new

    @pl.when(p == pl.num_programs(1) - 1)
    def _():
        lam = _lambda_value(lam_ref, lam_init)
        o = (acc_ref[0:heads, :] / l_ref[0:heads, :]
             - lam * (acc_ref[heads:2 * heads, :] / l_ref[heads:2 * heads, :]))
        o = o * lax.rsqrt(jnp.mean(o * o, axis=-1, keepdims=True) + RMS_EPS) * w_ref[...] * (1.0 - lam_init)
        o_ref[0] = o.astype(o_ref.dtype)


def attn_decode(q, k_new, v_new, cache_k, cache_v, page_table, lam_vec, subln_w, *, lam_init):
    b, heads, _ = q.shape
    n_pages = page_table.shape[1]
    page = cache_k.shape[1]
    pg = math.gcd(n_pages, DECODE_PAGES_PER_STEP)
    tok = pl.BlockSpec((1, heads, LANES), lambda i, p, pt: (i, 0, 0))

    def cache(j):
        return pl.BlockSpec((1, page, heads, LANES), lambda i, p, pt: (pt[i * n_pages + p * pg + j], 0, 0, 0))

    grid_spec = pltpu.PrefetchScalarGridSpec(
        num_scalar_prefetch=1,
        grid=(b, n_pages // pg),
        in_specs=[pl.BlockSpec(lam_vec.shape, lambda i, p, pt: (0, 0)), tok, tok, tok]
        + [cache(j) for j in range(pg)] * 2
        + [pl.BlockSpec((1, LANES), lambda i, p, pt: (0, 0))],
        out_specs=tok,
        scratch_shapes=[pltpu.VMEM((2 * heads, 1), F32), pltpu.VMEM((2 * heads, 1), F32),
                        pltpu.VMEM((2 * heads, LANES), F32)],
    )
    return pl.pallas_call(
        functools.partial(_attn_decode_kernel, lam_init=lam_init, pg=pg),
        out_shape=jax.ShapeDtypeStruct(q.shape, BF16),
        grid_spec=grid_spec,
        compiler_params=_params("parallel", "arbitrary"),
        name="attn_decode",
    )(page_table.reshape(-1), lam_vec, q, k_new, v_new, *([cache_k] * pg), *([cache_v] * pg),
      subln_w.reshape(1, LANES))


def _router_kernel(x_ref, sc_ref, sh_ref, wr_ref, br_ref, idx_ref, gate_ref, pos_ref, cnt_ref,
                   tri_ref, carry_ref, *, tm):
    i = pl.program_id(0)

    @pl.when(i == 0)
    def _():
        r = lax.broadcasted_iota(I32, (tm, tm), 0)
        c = lax.broadcasted_iota(I32, (tm, tm), 1)
        tri_ref[...] = jnp.where(c < r, 1.0, 0.0).astype(BF16)
        carry_ref[...] = jnp.zeros_like(carry_ref)

    h = x_ref[...] * (1.0 + sc_ref[0]) + sh_ref[0]
    h_hi = h.astype(BF16)
    h_lo = (h - h_hi.astype(F32)).astype(BF16)
    w = wr_ref[...]
    w_hi = w.astype(BF16)
    w_lo = (w - w_hi.astype(F32)).astype(BF16)
    logits = _dot(h_hi, w_hi) + _dot(h_lo, w_hi) + _dot(h_hi, w_lo) + br_ref[...]
    lane = lax.broadcasted_iota(I32, (1, LANES), 1)
    work = jnp.where(lane < N_EXPERTS, logits, NEG_BIG)
    tops, idxs = [], []
    member = jnp.zeros((tm, LANES), F32)
    for _ in range(TOP_K):
        mx = jnp.max(work, axis=1, keepdims=True)
        ix = jnp.min(jnp.where(work == mx, lane, LANES), axis=1, keepdims=True)
        hit = lane == ix
        member = jnp.where(hit, 1.0, member)
        work = jnp.where(hit, NEG_BIG, work)
        tops.append(mx)
        idxs.append(ix)
    ex = [jnp.exp(t - tops[0]) for t in tops]
    denom = ex[0] + ex[1] + ex[2] + ex[3]
    rank = _dot(tri_ref[...], member.astype(BF16)) + carry_ref[...]
    idx_out = jnp.zeros((tm, LANES), I32)
    gate_out = jnp.zeros((tm, LANES), F32)
    pos_out = jnp.zeros((tm, LANES), I32)
    for k in range(TOP_K):
        pk = jnp.sum(jnp.where(lane == idxs[k], rank, 0.0), axis=1, keepdims=True)
        idx_out = jnp.where(lane == k, idxs[k], idx_out)
        gate_out = jnp.where(lane == k, ex[k] / denom, gate_out)
        pos_out = jnp.where(lane == k, pk.astype(I32), pos_out)
    idx_ref[...] = idx_out
    gate_ref[...] = gate_out
    pos_ref[...] = pos_out
    carry_ref[...] = carry_ref[...] + jnp.sum(member, axis=0, keepdims=True)
    cnt_ref[...] = carry_ref[...].astype(I32)


def router(x, sc, sh, w_r, b_r, *, rows_per_group, tm):
    n, d = x.shape
    tiles_per_group = rows_per_group // tm
    r = sc.shape[1]
    w_pad = jnp.zeros((d, LANES), F32).at[:, :N_EXPERTS].set(w_r)
    b_pad = jnp.zeros((1, LANES), F32).at[0, :N_EXPERTS].set(b_r)
    mod = pl.BlockSpec((1, r, d), lambda i: (i // tiles_per_group, 0, 0))
    wide = pl.BlockSpec((tm, LANES), lambda i: (i, 0))
    return pl.pallas_call(
        functools.partial(_router_kernel, tm=tm),
        out_shape=(jax.ShapeDtypeStruct((n, LANES), I32), jax.ShapeDtypeStruct((n, LANES), F32),
                   jax.ShapeDtypeStruct((n, LANES), I32), jax.ShapeDtypeStruct((1, LANES), I32)),
        grid=(n // tm,),
        in_specs=[pl.BlockSpec((tm, d), lambda i: (i, 0)), mod, mod,
                  pl.BlockSpec((d, LANES), lambda i: (0, 0)), pl.BlockSpec((1, LANES), lambda i: (0, 0))],
        out_specs=(wide, wide, wide, pl.BlockSpec((1, LANES), lambda i: (0, 0))),
        scratch_shapes=[pltpu.VMEM((tm, tm), BF16), pltpu.VMEM((1, LANES), F32)],
        compiler_params=_params("arbitrary"),
        name="moe_router",
    )(x, sc, sh, w_pad, b_pad)


def _dispatch_kernel(gap_lo_ref, gap_hi_ref, x_ref, sc_ref, sh_ref, dest_ref, xs_ref,
                     h_ref, zero_ref, idx_ref, sem_ref, isem_ref, *, tm, n_gap_rows):
    i = pl.program_id(0)
    n_idx = tm * TOP_K
    idx_copy = pltpu.make_async_copy(dest_ref.at[0, 0], idx_ref, isem_ref)
    idx_copy.start()
    h = x_ref[...] * (1.0 + sc_ref[0]) + sh_ref[0]
    lg = h.shape[1] // LANES
    for g in range(lg):
        h_ref[pl.ds(g, tm, stride=lg), :] = h[:, g * LANES:(g + 1) * LANES]
    idx_copy.wait()

    def row_copy(src, src_tok, dst_tok):
        return pltpu.make_async_copy(_token_rows(src, src_tok, lg), _token_rows(xs_ref, dst_tok, lg), sem_ref)

    def issue(r, carry):
        for k in range(TOP_K):
            row_copy(h_ref, r, idx_ref[r * TOP_K + k]).start()
        return carry

    lax.fori_loop(0, tm, issue, 0, unroll=DMA_UNROLL)

    def drain(r, carry):
        row_copy(h_ref, 0, 0).wait()
        return carry

    lax.fori_loop(0, n_idx, drain, 0, unroll=DMA_UNROLL)

    @pl.when(i == 0)
    def _():
        zero_ref[...] = jnp.zeros_like(zero_ref)

        def per_expert(e, carry):
            def fill(rw, c2):
                row_copy(zero_ref, 0, rw).start()
                return c2
            return lax.fori_loop(gap_lo_ref[e], gap_hi_ref[e], fill, carry)

        lax.fori_loop(0, N_EXPERTS, per_expert, 0)

        def drain_zero(r, carry):
            row_copy(zero_ref, 0, 0).wait()
            return carry

        lax.fori_loop(0, n_gap_rows, drain_zero, 0)


def dispatch(x, sc, sh, dest, gap_lo, gap_hi, *, rows_per_group, tm, n_slots_padded):
    n, d = x.shape
    tiles_per_group = rows_per_group // tm
    r = sc.shape[1]
    n_idx = tm * TOP_K
    mod = pl.BlockSpec((1, r, d), lambda i, lo, hi: (i // tiles_per_group, 0, 0))
    grid_spec = pltpu.PrefetchScalarGridSpec(
        num_scalar_prefetch=2,
        grid=(n // tm,),
        in_specs=[pl.BlockSpec((tm, d), lambda i, lo, hi: (i, 0)), mod, mod,
                  pl.BlockSpec((1, 1, n_idx), lambda i, lo, hi: (i, 0, 0))],
        out_specs=pl.BlockSpec(memory_space=pl.ANY),
        scratch_shapes=[pltpu.VMEM((tm * (d // LANES), LANES), F32), pltpu.VMEM((d // LANES, LANES), F32),
                        pltpu.SMEM((n_idx,), I32), pltpu.SemaphoreType.DMA, pltpu.SemaphoreType.DMA],
    )
    return pl.pallas_call(
        functools.partial(_dispatch_kernel, tm=tm, n_gap_rows=n_slots_padded - n * TOP_K),
        out_shape=jax.ShapeDtypeStruct((n_slots_padded * (d // LANES), LANES), F32),
        grid_spec=grid_spec,
        compiler_params=_params("arbitrary"),
        name="moe_dispatch",
    )(gap_lo, gap_hi, x, sc, sh, dest.reshape(n // tm, 1, n_idx))


def _expert_ffn_kernel(be_ref, xs_ref, wgu_ref, bgu_ref, wd_ref, bd_ref, ys_ref, wgu16_ref, wd16_ref, *, blk):
    i = pl.program_id(0)

    @pl.when((i == 0) | (be_ref[i] != be_ref[jnp.maximum(i - 1, 0)]))
    def _():
        half = GU_GROUP // 2
        r = lax.broadcasted_iota(I32, (GU_GROUP, GU_GROUP), 0)
        c = lax.broadcasted_iota(I32, (GU_GROUP, GU_GROUP), 1)
        perm = jnp.where(r == jnp.where(c < half, 2 * c, 2 * (c - half) + 1), 1.0, 0.0).astype(BF16)
        for g in range(wgu_ref.shape[2] // GU_GROUP):
            cols = slice(g * GU_GROUP, (g + 1) * GU_GROUP)
            wgu16_ref[:, cols] = _dot(wgu_ref[0, :, cols].astype(BF16), perm).astype(BF16)
        wd16_ref[...] = wd_ref[0].astype(BF16)

    lg = xs_ref.shape[0] // blk
    x = jnp.concatenate([xs_ref[pl.ds(g, blk, stride=lg), :] for g in range(lg)], axis=1).astype(BF16)
    gu = _dot(x, wgu16_ref[...]) + bgu_ref[0]
    half = GU_GROUP // 2
    acts = []
    for g in range(gu.shape[1] // GU_GROUP):
        gate = jnp.minimum(gu[:, g * GU_GROUP:g * GU_GROUP + half], SWIGLU_LIMIT)
        up = jnp.clip(gu[:, g * GU_GROUP + half:(g + 1) * GU_GROUP], -SWIGLU_LIMIT, SWIGLU_LIMIT)
        acts.append((up + 1.0) * (gate * _sigmoid(gate * SWIGLU_ALPHA)))
    act = jnp.concatenate(acts, axis=1).astype(BF16)
    y = _dot(act, wd16_ref[...]) + bd_ref[0]
    for g in range(lg):
        ys_ref[pl.ds(g, blk, stride=lg), :] = y[:, g * LANES:(g + 1) * LANES]


def expert_ffn(xs, block_expert, wgu, bgu, wd, bd, *, blk, expert_base):
    d, de2 = wgu.shape[1], wgu.shape[2]
    lg = d // LANES
    ns = xs.shape[0] // lg
    rows = pl.BlockSpec((blk * lg, LANES), lambda i, be: (i, 0))
    wspec = lambda a, b: pl.BlockSpec((1, a, b), lambda i, be: (expert_base + be[i], 0, 0))
    grid_spec = pltpu.PrefetchScalarGridSpec(
        num_scalar_prefetch=1,
        grid=(ns // blk,),
        in_specs=[rows, wspec(d, de2), wspec(1, de2), wspec(de2 // 2, d), wspec(1, d)],
        out_specs=rows,
        scratch_shapes=[pltpu.VMEM((d, de2), BF16), pltpu.VMEM((de2 // 2, d), BF16)],
    )
    return pl.pallas_call(
        functools.partial(_expert_ffn_kernel, blk=blk),
        out_shape=jax.ShapeDtypeStruct(xs.shape, F32),
        grid_spec=grid_spec,
        compiler_params=_params("arbitrary"),
        name="moe_expert_ffn",
    )(block_expert, xs, wgu, bgu, wd, bd)


def _combine_ln_kernel(x_ref, gm_ref, gate_ref, dest_ref, ys_ref, lg_ref, lb_ref, o_ref,
                       buf_ref, idx_ref, sem_ref, isem_ref, *, tm, alpha):
    n_idx = tm * TOP_K
    idx_copy = pltpu.make_async_copy(dest_ref.at[0, 0], idx_ref, isem_ref)
    idx_copy.start()
    idx_copy.wait()

    lg = buf_ref.shape[1] // tm

    def row_copy(src_tok, k, r):
        return pltpu.make_async_copy(_token_rows(ys_ref, src_tok, lg), _token_rows(buf_ref.at[k], r, lg), sem_ref)

    def issue(r, carry):
        for k in range(TOP_K):
            row_copy(idx_ref[r * TOP_K + k], k, r).start()
        return carry

    lax.fori_loop(0, tm, issue, 0, unroll=DMA_UNROLL)

    def drain(r, carry):
        row_copy(0, 0, 0).wait()
        return carry

    lax.fori_loop(0, n_idx, drain, 0, unroll=DMA_UNROLL)
    gates = gate_ref[...]
    cols = []
    for g in range(lg):
        yg = gates[:, 0:1] * buf_ref[0, pl.ds(g, tm, stride=lg), :]
        for k in range(1, TOP_K):
            yg = yg + gates[:, k:k + 1] * buf_ref[k, pl.ds(g, tm, stride=lg), :]
        cols.append(yg)
    y = jnp.concatenate(cols, axis=1)
    o_ref[...] = _layer_norm(alpha * x_ref[...] + gm_ref[0] * y, lg_ref[...], lb_ref[...])


def combine_ln(x, gm, gates, dest, ys, ln_g, ln_b, *, rows_per_group, tm, alpha):
    n, d = x.shape
    tiles_per_group = rows_per_group // tm
    r = gm.shape[1]
    n_idx = tm * TOP_K
    row = pl.BlockSpec((tm, d), lambda i: (i, 0))
    vec = pl.BlockSpec((1, d), lambda i: (0, 0))
    return pl.pallas_call(
        functools.partial(_combine_ln_kernel, tm=tm, alpha=alpha),
        out_shape=jax.ShapeDtypeStruct((n, d), F32),
        grid=(n // tm,),
        in_specs=[row, pl.BlockSpec((1, r, d), lambda i: (i // tiles_per_group, 0, 0)),
                  pl.BlockSpec((tm, LANES), lambda i: (i, 0)),
                  pl.BlockSpec((1, 1, n_idx), lambda i: (i, 0, 0)),
                  pl.BlockSpec(memory_space=pl.ANY), vec, vec],
        out_specs=row,
        scratch_shapes=[pltpu.VMEM((TOP_K, tm * (d // LANES), LANES), F32), pltpu.SMEM((n_idx,), I32),
                        pltpu.SemaphoreType.DMA, pltpu.SemaphoreType.DMA],
        compiler_params=_params("arbitrary"),
        name="moe_combine_ln",
    )(x, gm, gates, dest.reshape(n // tm, 1, n_idx), ys, ln_g.reshape(1, d), ln_b.reshape(1, d))


def _moe_layer(x, sc, sh, gm, ln_g, ln_b, w_r, b_r, wgu, bgu, wd, bd, *, expert_base, rows_per_group, tm, blk,
               alpha):
    n, d = x.shape
    idx_w, gate_w, pos_w, cnt_w = router(x, sc, sh, w_r, b_r, rows_per_group=rows_per_group, tm=tm)
    counts = cnt_w[0, :N_EXPERTS]
    top_idx, pos = idx_w[:, :TOP_K], pos_w[:, :TOP_K]
    n_slots = n * TOP_K
    n_blocks = -(-(n_slots + N_EXPERTS * (blk - 1)) // blk)
    padded = (counts + blk - 1) // blk * blk
    pad_end = jnp.cumsum(padded)
    pad_start = pad_end - padded
    onehot = top_idx[..., None] == jnp.arange(N_EXPERTS, dtype=I32)
    dest = (jnp.sum(jnp.where(onehot, pad_start, 0), axis=-1) + pos).astype(I32)
    block_start = jnp.arange(n_blocks, dtype=I32) * blk
    block_expert = jnp.minimum(
        jnp.sum((pad_end[None, :] <= block_start[:, None]).astype(I32), axis=1), N_EXPERTS - 1).astype(I32)
    gap_lo = (pad_start + counts).astype(I32)
    gap_hi = jnp.concatenate([pad_start[1:], jnp.full((1,), n_blocks * blk, I32)]).astype(I32)
    xs = dispatch(x, sc, sh, dest, gap_lo, gap_hi, rows_per_group=rows_per_group, tm=tm,
                  n_slots_padded=n_blocks * blk)
    ys = expert_ffn(xs, block_expert, wgu, bgu, wd, bd, blk=blk, expert_base=expert_base)
    return combine_ln(x, gm, gate_w, dest, ys, ln_g, ln_b, rows_per_group=rows_per_group, tm=tm, alpha=alpha)


def _rope_tables(pos):
    inv_freq = ROPE_THETA ** (-jnp.arange(ROPE_HALF, dtype=F32) / ROPE_HALF)
    ang = pos.astype(F32)[:, None] * inv_freq[None, :]
    cos, sin = jnp.cos(ang), jnp.sin(ang)
    reps = LANES // DIFF_HEAD_DIM
    return (jnp.concatenate([cos, cos] * reps, axis=1), jnp.concatenate([-sin, sin] * reps, axis=1))


def _trunk(x, mods, kv_mods, past_len, s0, cache, wts, *, batch, seq, tm, blk):
    n, d = x.shape
    depth = wts["ada_w"].shape[0]
    n_a = wts["hgrn_w_in"].shape[0]
    alpha = (2 * depth) ** 0.25
    rows_per_group = seq if seq >= tm else n
    kw = dict(rows_per_group=rows_per_group, tm=tm)
    decode = seq == 1
    pos = past_len + (jnp.zeros((n,), I32) if decode else jnp.arange(seq, dtype=I32))
    rope = _rope_tables(pos)
    heads = d // LANES
    states = []
    k_new = v_new = k16 = v16 = None
    for l in range(depth):
        sh_m, sc_m, g_m, sh_f, sc_f, g_f = mods[l]
        if l < n_a:
            proj, = modmm(x, sc_m, sh_m, wts["hgrn_w_in"][l], out_dtypes=(F32,), **kw)
            lbs = [a[l][None] for a in wts["lb"]]
            nw = wts["hgrn_norm_w"][l].reshape(1, LANES)
            if decode:
                o, s_l = hgrn_step(proj, *lbs, nw, s0[l])
            else:
                o, s_l = hgrn_chunked(proj, *lbs, nw, s0[l], batch=batch, seq=seq, tb=min(seq, 512))
            states.append(s_l)
            w_out = wts["hgrn_w_out"][l]
        else:
            j = l - n_a
            lam_init = 0.8 - 0.6 * math.exp(-0.3 * l)
            if l == n_a:
                kv_shift, kv_scale = kv_mods
                half = wts["kv_w"].shape[1] // 2
                k_new, k16 = modmm(x, kv_scale, kv_shift, wts["kv_w"][:, :half], out_dtypes=(F32, BF16),
                                   rope=rope, **kw)
                v_new, v16 = modmm(x, kv_scale, kv_shift, wts["kv_w"][:, half:], out_dtypes=(F32, BF16), **kw)
            q_scale = DIFF_HEAD_DIM ** -0.5
            if decode:
                q, = modmm(x, sc_m, sh_m, wts["diff_w_q"][j], out_dtypes=(F32,), rope=rope, out_scale=q_scale, **kw)
                o = attn_decode(q.reshape(n, heads, LANES), k_new.reshape(n, heads, LANES),
                                v_new.reshape(n, heads, LANES), cache[0], cache[1], cache[2],
                                wts["diff_lambda"][j], wts["diff_subln_w"][j], lam_init=lam_init)
                o = o.reshape(n, d)
            else:
                q, = modmm(x, sc_m, sh_m, wts["diff_w_q"][j], out_dtypes=(BF16,), rope=rope, out_scale=q_scale, **kw)
                o = attn_prefill(q, k16, v16, wts["diff_lambda"][j], wts["diff_subln_w"][j],
                                 batch=batch, seq=seq, tq=min(seq, 1024), lam_init=lam_init, hpb=1)
            w_out = wts["diff_w_out"][j]
        x = proj_ln(o, w_out, x, g_m, wts["ln_g"][l, 0], wts["ln_b"][l, 0], alpha=alpha, **kw)
        x = _moe_layer(x, sc_f, sh_f, g_f, wts["ln_g"][l, 1], wts["ln_b"][l, 1],
                       wts["moe_w_router"][l], wts["moe_b_router"][l],
                       wts["moe_w_gu"], wts["moe_b_gu"], wts["moe_w_down"], wts["moe_b_down"],
                       expert_base=l * N_EXPERTS, alpha=alpha, blk=blk, **kw)
    return x, k_new, v_new, jnp.stack(states)


def kernel(x_prompt, x_sample, cache_k, cache_v, state_hgrn, page_table, c_prompt, c_sample, ada_w, ada_b, ln_g, ln_b, hgrn_w_in, hgrn_lb_logits, hgrn_norm_w, hgrn_w_out, kv_ada_w, kv_ada_b, kv_w, diff_w_q, diff_lambda, diff_subln_w, diff_w_out, moe_w_router, moe_b_router, moe_w_gu, moe_b_gu, moe_w_down, moe_b_down):
    b_p, t_p, d = x_prompt.shape
    b_d, t_d, _ = x_sample.shape
    depth = ada_w.shape[0]
    n_a = hgrn_w_in.shape[0]
    heads = d // LANES
    de = moe_w_down.shape[2]

    m_rows = -(-(b_d + b_p) // 8) * 8
    c_all = jnp.zeros((m_rows, d), F32).at[:b_d].set(c_sample).at[b_d:b_d + b_p].set(c_prompt)
    mod_all = adaln(c_all, ada_w, ada_b)
    kv_mod_all = adaln(c_all, kv_ada_w[None], kv_ada_b[None])[0]

    def split_mods(m, n_vec, lo, hi, per_row):
        parts = [m[lo:hi, v * d:(v + 1) * d] for v in range(n_vec)]
        return [p[None] if per_row else p[:, None, :] for p in parts]

    mods_p = [split_mods(mod_all[l], 6, b_d, b_d + b_p, False) for l in range(depth)]
    mods_s = [split_mods(mod_all[l], 6, 0, b_d, True) for l in range(depth)]
    kv_mods_p = split_mods(kv_mod_all, 2, b_d, b_d + b_p, False)
    kv_mods_s = split_mods(kv_mod_all, 2, 0, b_d, True)

    loglb, log1m, onem = hgrn_lower_bounds(hgrn_lb_logits)

    n_exp = moe_w_gu.shape[1]
    half = GU_GROUP // 2
    b_gu = moe_b_gu.reshape(depth, n_exp, 2 * de // GU_GROUP, half, 2)
    b_gu = jnp.swapaxes(b_gu, -1, -2).reshape(depth * n_exp, 1, 2 * de)

    wts = dict(
        ada_w=ada_w, ln_g=ln_g, ln_b=ln_b,
        hgrn_w_in=hgrn_w_in.astype(BF16), lb=(loglb, log1m, onem), hgrn_norm_w=hgrn_norm_w,
        hgrn_w_out=hgrn_w_out.astype(BF16), kv_w=kv_w.astype(BF16), diff_w_q=diff_w_q.astype(BF16),
        diff_lambda=diff_lambda, diff_subln_w=diff_subln_w, diff_w_out=diff_w_out.astype(BF16),
        moe_w_router=moe_w_router, moe_b_router=moe_b_router,
        moe_w_gu=moe_w_gu.reshape(depth * n_exp, d, 2 * de), moe_b_gu=b_gu,
        moe_w_down=moe_w_down.reshape(depth * n_exp, de, d), moe_b_down=moe_b_down.reshape(depth * n_exp, 1, d),
    )

    s0_prompt = jnp.zeros((n_a, b_p, heads, LANES, LANES), F32)
    y_p, k_p, v_p, st_p = _trunk(x_prompt.reshape(b_p * t_p, d), mods_p, kv_mods_p, 0, s0_prompt, None, wts,
                                 batch=b_p, seq=t_p, tm=min(512, t_p), blk=256)
    past_len = page_table.shape[1] * cache_k.shape[1]
    y_s, k_s, v_s, st_s = _trunk(x_sample.reshape(b_d * t_d, d), mods_s, kv_mods_s, past_len, state_hgrn,
                                 (cache_k, cache_v, page_table), wts, batch=b_d, seq=t_d, tm=b_d, blk=16)
    dv = d // heads
    return (y_p.reshape(b_p, t_p, d), y_s.reshape(b_d, t_d, d),
            k_p.reshape(b_p, t_p, heads, dv), v_p.reshape(b_p, t_p, heads, dv), st_p,
            k_s.reshape(b_d, t_d, heads, dv), v_s.reshape(b_d, t_d, heads, dv), st_s)
```

```python
import functools
import math

import jax
import jax.numpy as jnp
from jax import lax
from jax.experimental import pallas as pl
from jax.experimental.pallas import tpu as pltpu

F32 = jnp.float32
BF16 = jnp.bfloat16
I32 = jnp.int32

LANES = 128
HGRN_CHUNK = 64
HGRN_SUB = 16
LB_FLOOR = 1e-30
LOG2E = math.log2(math.e)
DIFF_HEAD_DIM = 64
ROPE_THETA = 10000.0
ROPE_HALF = DIFF_HEAD_DIM // 2
NEG_BIG = -1e30
N_EXPERTS = 32
TOP_K = 4
SWIGLU_LIMIT = 7.0
SWIGLU_ALPHA = 1.702
LN_EPS = 1e-5
RMS_EPS = 1e-5
VMEM_LIMIT = 56 * 1024 * 1024
DECODE_PAGES_PER_STEP = 8
GU_GROUP = 256
DMA_UNROLL = 8


def _params(*sem):
    return pltpu.CompilerParams(dimension_semantics=sem, vmem_limit_bytes=VMEM_LIMIT)


def _sigmoid(x):
    return 1.0 / (1.0 + jnp.exp(-x))


def _silu(x):
    return x * _sigmoid(x)


def _log_sigmoid(x):
    return jnp.minimum(x, 0.0) - jnp.log1p(jnp.exp(-jnp.abs(x)))


def _layer_norm(z, g, b):
    mu = jnp.mean(z, axis=-1, keepdims=True)
    zc = z - mu
    var = jnp.mean(zc * zc, axis=-1, keepdims=True)
    return zc * lax.rsqrt(var + LN_EPS) * g + b


def _dot(a, b):
    return jnp.dot(a, b, preferred_element_type=F32)


def _dot_nt(a, b):
    return lax.dot_general(a, b, (((1,), (1,)), ((), ())), preferred_element_type=F32)


def _dot_tn(a, b):
    return lax.dot_general(a, b, (((0,), (0,)), ((), ())), preferred_element_type=F32)


def _token_rows(ref, tok, lg):
    start = tok * lg
    return ref.at[pl.ds(start if isinstance(start, int) else pl.multiple_of(start, lg), lg)]


def _split3(x):
    hi = x.astype(BF16)
    r1 = x - hi.astype(F32)
    mid = r1.astype(BF16)
    lo = (r1 - mid.astype(F32)).astype(BF16)
    return hi, mid, lo


def _adaln_kernel(c_ref, w_ref, b_ref, o_ref):
    s = _silu(c_ref[...]).astype(BF16)
    o_ref[0] = _dot(s, w_ref[0].astype(BF16)) + b_ref[0]


def adaln(c, w, b, tn=1024):
    m, d = c.shape
    nl, _, no = w.shape
    return pl.pallas_call(
        _adaln_kernel,
        out_shape=jax.ShapeDtypeStruct((nl, m, no), F32),
        grid=(nl, no // tn),
        in_specs=[
            pl.BlockSpec((m, d), lambda l, j: (0, 0)),
            pl.BlockSpec((1, d, tn), lambda l, j: (l, 0, j)),
            pl.BlockSpec((1, 1, tn), lambda l, j: (l, 0, j)),
        ],
        out_specs=pl.BlockSpec((1, m, tn), lambda l, j: (l, 0, j)),
        compiler_params=_params("parallel", "parallel"),
        name="adaln",
    )(c, w, b.reshape(nl, 1, no))


def _lb_kernel(logit_ref, loglb_ref, log1m_ref, onem_ref):
    x = logit_ref[...]
    nl = x.shape[0]
    e = jnp.exp(x - jnp.max(x, axis=0, keepdims=True))
    sm = e / jnp.sum(e, axis=0, keepdims=True)
    acc = jnp.zeros_like(sm[0:1])
    for l in range(nl):
        lb = jnp.clip(acc, 0.0, 1.0 - 1e-6)
        loglb_ref[l] = jnp.log(jnp.maximum(lb, LB_FLOOR))
        log1m_ref[l] = jnp.log1p(-lb)
        onem_ref[l] = 1.0 - lb
        acc = acc + sm[l:l + 1]


def hgrn_lower_bounds(lb_logits):
    nl, f = lb_logits.shape
    shp = jax.ShapeDtypeStruct((nl, 1, f), F32)
    return pl.pallas_call(_lb_kernel, out_shape=(shp, shp, shp), name="hgrn_lb")(lb_logits)


def _rotary_tile(y, cos, sin_signed):
    lane = lax.broadcasted_iota(I32, (1, LANES), 1)
    first_half = (lane % DIFF_HEAD_DIM) < ROPE_HALF
    partner = jnp.where(first_half, pltpu.roll(y, LANES - ROPE_HALF, 1), pltpu.roll(y, ROPE_HALF, 1))
    return y * cos + partner * sin_signed


def _modmm_kernel(*refs, rotary, out_scale, n_out):
    if rotary:
        x_ref, sc_ref, sh_ref, w_ref, cos_ref, sin_ref = refs[:6]
        rest = refs[6:]
    else:
        x_ref, sc_ref, sh_ref, w_ref = refs[:4]
        rest = refs[4:]
    out_refs, h_ref = rest[:n_out], rest[n_out]

    @pl.when(pl.program_id(1) == 0)
    def _():
        h_ref[...] = (x_ref[...] * (1.0 + sc_ref[0]) + sh_ref[0]).astype(BF16)

    y = _dot(h_ref[...], w_ref[...])
    if rotary:
        cos, sin = cos_ref[...], sin_ref[...]
        y = jnp.concatenate(
            [_rotary_tile(y[:, g * LANES:(g + 1) * LANES], cos, sin) for g in range(y.shape[1] // LANES)], axis=1)
    if out_scale != 1.0:
        y = y * out_scale
    for o_ref in out_refs:
        o_ref[...] = y.astype(o_ref.dtype)


def modmm(x, sc, sh, w, *, rows_per_group, tm, out_dtypes, rope=None, out_scale=1.0, tn=1024):
    n, d = x.shape
    no = w.shape[1]
    tiles_per_group = rows_per_group // tm
    r = sc.shape[1]
    in_specs = [
        pl.BlockSpec((tm, d), lambda i, j: (i, 0)),
        pl.BlockSpec((1, r, d), lambda i, j: (i // tiles_per_group, 0, 0)),
        pl.BlockSpec((1, r, d), lambda i, j: (i // tiles_per_group, 0, 0)),
        pl.BlockSpec((d, tn), lambda i, j: (0, j)),
    ]
    args = [x, sc, sh, w]
    if rope is not None:
        pos_tiles = rope[0].shape[0] // tm
        in_specs += [pl.BlockSpec((tm, LANES), lambda i, j: (i % pos_tiles, 0))] * 2
        args += list(rope)
    outs = tuple(jax.ShapeDtypeStruct((n, no), dt) for dt in out_dtypes)
    res = pl.pallas_call(
        functools.partial(_modmm_kernel, rotary=rope is not None, out_scale=out_scale, n_out=len(outs)),
        out_shape=outs,
        grid=(n // tm, no // tn),
        in_specs=in_specs,
        out_specs=tuple(pl.BlockSpec((tm, tn), lambda i, j: (i, j)) for _ in outs),
        scratch_shapes=[pltpu.VMEM((tm, d), BF16)],
        compiler_params=_params("parallel", "arbitrary"),
        name="modmm",
    )(*args)
    return res


def _hgrn_head_chunk(q_raw, f, v, g_raw, loglb, log1m, onem, nw, st, tri, sub_row):
    c, sub = HGRN_CHUNK, HGRN_SUB
    half = sub // 2
    q = _silu(q_raw)
    x2 = log1m + _log_sigmoid(f)
    mx = jnp.maximum(loglb, x2)
    logf = mx + jnp.log1p(jnp.exp(jnp.minimum(loglb, x2) - mx))
    kk = onem * _sigmoid(-f)
    hi, mid, lo = _split3(logf)
    b = (_dot(tri, hi) + _dot(tri, mid) + _dot(tri, lo)) * LOG2E
    v16 = v.astype(BF16)
    o_inter = _dot_nt((q * jnp.exp2(b)).astype(BF16), st.astype(BF16))
    o_parts = []
    for i in range(c // sub):
        sl = slice(i * sub, (i + 1) * sub)
        qi, bi, ki, vi = q[sl], b[sl], kk[sl], v[sl]
        oi = o_inter[sl]
        top, bot = oi[:half], oi[half:]
        for s in range(sub):
            k_s, b_s, v_s = ki[s:s + 1], bi[s:s + 1], vi[s:s + 1]
            a_bot = jnp.sum(qi[half:] * k_s * jnp.exp2(bi[half:] - b_s), axis=1, keepdims=True)
            if s < half:
                a_top = jnp.sum(qi[:half] * k_s * jnp.exp2(bi[:half] - b_s), axis=1, keepdims=True)
                top = top + jnp.where(sub_row >= s, a_top, 0.0) * v_s
            else:
                a_bot = jnp.where(sub_row >= s - half, a_bot, 0.0)
            bot = bot + a_bot * v_s
        oi = jnp.concatenate([top, bot], axis=0)
        if i > 0:
            r = b[i * sub - 1:i * sub]
            qs = (qi * jnp.exp2(bi - r)).astype(BF16)
            ks = (kk[:i * sub] * jnp.exp2(r - b[:i * sub])).astype(BF16)
            a = _dot_nt(qs, ks)
            oi = oi + _dot(a.astype(BF16), v16[:i * sub])
        o_parts.append(oi)
    o = jnp.concatenate(o_parts, axis=0)
    b_last = b[c - 1:c]
    kd = (kk * jnp.exp2(b_last - b)).astype(BF16)
    st_new = st * jnp.exp2(b_last) + _dot_tn(v16, kd)
    o = o * lax.rsqrt(jnp.mean(o * o, axis=-1, keepdims=True) + RMS_EPS) * nw * _silu(g_raw)
    return o, st_new


def _hgrn_chunk_kernel(q_ref, f_ref, v_ref, g_ref, loglb_ref, log1m_ref, onem_ref, nw_ref, s0_ref,
                       o_ref, sfin_ref, st_ref, *, n_chunks, hg):
    t = pl.program_id(2)
    c = HGRN_CHUNK

    @pl.when(t == 0)
    def _():
        for hh in range(hg):
            st_ref[hh] = s0_ref[0, hh].T

    loglb, log1m, onem, nw = loglb_ref[0], log1m_ref[0], onem_ref[0], nw_ref[...]
    row = lax.broadcasted_iota(I32, (c, c), 0)
    col = lax.broadcasted_iota(I32, (c, c), 1)
    tri = jnp.where(col <= row, 1.0, 0.0).astype(BF16)
    sub_row = lax.broadcasted_iota(I32, (HGRN_SUB // 2, 1), 0)

    def chunk(ci, carry):
        rows = pl.ds(pl.multiple_of(ci * c, c), c)
        outs = []
        for hh in range(hg):
            ln = slice(hh * LANES, (hh + 1) * LANES)
            o, st_new = _hgrn_head_chunk(q_ref[rows, ln], f_ref[rows, ln], v_ref[rows, ln], g_ref[rows, ln],
                                         loglb[:, ln], log1m[:, ln], onem[:, ln], nw, st_ref[hh], tri, sub_row)
            st_ref[hh] = st_new
            outs.append(o)
        o_ref[rows, :] = jnp.concatenate(outs, axis=1).astype(o_ref.dtype)
        return carry

    lax.fori_loop(0, n_chunks, chunk, 0)

    @pl.when(t == pl.num_programs(2) - 1)
    def _():
        for hh in range(hg):
            sfin_ref[0, hh] = st_ref[hh].T


def hgrn_chunked(proj, loglb, log1m, onem, norm_w, s0, *, batch, seq, tb, hg=8):
    n = proj.shape[0]
    heads = proj.shape[1] // (4 * LANES)
    tpb = seq // tb
    ngrp = heads // hg
    w = hg * LANES
    row_blk = lambda off: pl.BlockSpec((tb, w), lambda b, h, t: (b * tpb + t, off * ngrp + h))
    vec = pl.BlockSpec((1, 1, w), lambda b, h, t: (0, 0, h))
    st_spec = pl.BlockSpec((1, hg, LANES, LANES), lambda b, h, t: (b, h, 0, 0))
    return pl.pallas_call(
        functools.partial(_hgrn_chunk_kernel, n_chunks=tb // HGRN_CHUNK, hg=hg),
        out_shape=(jax.ShapeDtypeStruct((n, heads * LANES), BF16),
                   jax.ShapeDtypeStruct(s0.shape, F32)),
        grid=(batch, ngrp, tpb),
        in_specs=[row_blk(0), row_blk(1), row_blk(2), row_blk(3), vec, vec, vec,
                  pl.BlockSpec((1, LANES), lambda b, h, t: (0, 0)), st_spec],
        out_specs=(pl.BlockSpec((tb, w), lambda b, h, t: (b * tpb + t, h)), st_spec),
        scratch_shapes=[pltpu.VMEM((hg, LANES, LANES), F32)],
        compiler_params=_params("parallel", "parallel", "arbitrary"),
        name="hgrn_chunked",
    )(proj, proj, proj, proj, loglb, log1m, onem, norm_w, s0)


def _hgrn_step_kernel(q_ref, f_ref, v_ref, g_ref, loglb_ref, log1m_ref, onem_ref, nw_ref, s_ref,
                      o_ref, snew_ref, *, bb):
    f = f_ref[...]
    q = _silu(q_ref[...])
    v = v_ref[...]
    x1 = loglb_ref[0]
    x2 = log1m_ref[0] + _log_sigmoid(f)
    mx = jnp.maximum(x1, x2)
    decay = jnp.exp(mx + jnp.log1p(jnp.exp(jnp.minimum(x1, x2) - mx)))
    kk = onem_ref[0] * _sigmoid(-f)
    q_t, d_t, k_t = q.T, decay.T, kk.T
    rows = []
    for e in range(bb):
        s_new = s_ref[e, 0] * d_t[:, e:e + 1] + k_t[:, e:e + 1] * v[e:e + 1, :]
        snew_ref[e, 0] = s_new
        rows.append(jnp.sum(q_t[:, e:e + 1] * s_new, axis=0, keepdims=True))
    o = jnp.concatenate(rows, axis=0)
    o = o * lax.rsqrt(jnp.mean(o * o, axis=-1, keepdims=True) + RMS_EPS) * nw_ref[...] * _silu(g_ref[...])
    o_ref[...] = o.astype(o_ref.dtype)


def hgrn_step(proj, loglb, log1m, onem, norm_w, state, *, bb=16):
    n = proj.shape[0]
    heads = proj.shape[1] // (4 * LANES)
    row_blk = lambda off: pl.BlockSpec((bb, LANES), lambda i, h: (i, off * heads + h))
    vec = pl.BlockSpec((1, 1, LANES), lambda i, h: (0, 0, h))
    st_spec = pl.BlockSpec((bb, 1, LANES, LANES), lambda i, h: (i, h, 0, 0))
    return pl.pallas_call(
        functools.partial(_hgrn_step_kernel, bb=bb),
        out_shape=(jax.ShapeDtypeStruct((n, heads * LANES), BF16),
                   jax.ShapeDtypeStruct(state.shape, F32)),
        grid=(n // bb, heads),
        in_specs=[row_blk(0), row_blk(1), row_blk(2), row_blk(3), vec, vec, vec,
                  pl.BlockSpec((1, LANES), lambda i, h: (0, 0)), st_spec],
        out_specs=(pl.BlockSpec((bb, LANES), lambda i, h: (i, h)), st_spec),
        compiler_params=_params("parallel", "parallel"),
        name="hgrn_step",
    )(proj, proj, proj, proj, loglb, log1m, onem, norm_w, state)


def _proj_ln_kernel(a_ref, w_ref, x_ref, gm_ref, lg_ref, lb_ref, o_ref, *, alpha):
    y = _dot(a_ref[...], w_ref[...])
    o_ref[...] = _layer_norm(alpha * x_ref[...] + gm_ref[0] * y, lg_ref[...], lb_ref[...])


def proj_ln(a, w, x, gm, ln_g, ln_b, *, rows_per_group, tm, alpha):
    n, d = x.shape
    tiles_per_group = rows_per_group // tm
    r = gm.shape[1]
    row = pl.BlockSpec((tm, d), lambda i: (i, 0))
    vec = pl.BlockSpec((1, d), lambda i: (0, 0))
    return pl.pallas_call(
        functools.partial(_proj_ln_kernel, alpha=alpha),
        out_shape=jax.ShapeDtypeStruct((n, d), F32),
        grid=(n // tm,),
        in_specs=[row, pl.BlockSpec((d, d), lambda i: (0, 0)), row,
                  pl.BlockSpec((1, r, d), lambda i: (i // tiles_per_group, 0, 0)), vec, vec],
        out_specs=row,
        compiler_params=_params("parallel"),
        name="proj_ln",
    )(a, w, x, gm, ln_g.reshape(1, d), ln_b.reshape(1, d))


def _lambda_value(lam_ref, lam_init):
    lv = lam_ref[...]
    s1 = jnp.sum(lv[0:1] * lv[1:2], axis=1, keepdims=True)
    s2 = jnp.sum(lv[2:3] * lv[3:4], axis=1, keepdims=True)
    return jnp.exp(s1) - jnp.exp(s2) + lam_init


def _attn_prefill_kernel(lam_ref, q_ref, k_ref, v_ref, w_ref, o_ref, *, tq, lam_init, hpb):
    i = pl.program_id(2)
    lane = lax.broadcasted_iota(I32, (1, LANES), 1)
    keep = lax.broadcasted_iota(I32, (tq, tq), 1) <= lax.broadcasted_iota(I32, (tq, tq), 0)
    q_maps = []
    for hh in range(hpb):
        q = q_ref[:, hh * LANES:(hh + 1) * LANES]
        zero = jnp.zeros_like(q)
        q_maps += [jnp.where(lane < DIFF_HEAD_DIM, q, zero), jnp.where(lane < DIFF_HEAD_DIM, zero, q)]

    def update(s, vj, m, l, acc):
        m_new = jnp.maximum(m, jnp.max(s, axis=1, keepdims=True))
        alpha = jnp.exp(m - m_new)
        p = jnp.exp(s - m_new)
        l = alpha * l + jnp.sum(p, axis=1, keepdims=True)
        acc = alpha * acc + _dot(p.astype(BF16), vj)
        return m_new, l, acc

    def block(j, carry, masked):
        rows = pl.ds(pl.multiple_of(j * tq, tq), tq)
        out = ()
        for hh in range(hpb):
            ln = slice(hh * LANES, (hh + 1) * LANES)
            kj, vj = k_ref[rows, ln], v_ref[rows, ln]
            for c in range(2):
                s = _dot_nt(q_maps[2 * hh + c], kj)
                if masked:
                    s = jnp.where(keep, s, NEG_BIG)
                st = 3 * (2 * hh + c)
                out += update(s, vj, *carry[st:st + 3])
        return out

    m0 = jnp.full((tq, 1), NEG_BIG, F32)
    l0 = jnp.zeros((tq, 1), F32)
    a0 = jnp.zeros((tq, LANES), F32)
    carry = lax.fori_loop(0, i, lambda j, c: block(j, c, False), (m0, l0, a0) * (2 * hpb))
    fin = block(i, carry, True)
    lam = _lambda_value(lam_ref, lam_init)
    outs = []
    for hh in range(hpb):
        _, l1, a1, _, l2, a2 = fin[6 * hh:6 * hh + 6]
        o = a1 / l1 - lam * (a2 / l2)
        outs.append(o * lax.rsqrt(jnp.mean(o * o, axis=-1, keepdims=True) + RMS_EPS) * w_ref[...] * (1.0 - lam_init))
    o_ref[...] = jnp.concatenate(outs, axis=1).astype(o_ref.dtype)


def attn_prefill(q, k, v, lam_vec, subln_w, *, batch, seq, tq, lam_init, hpb=2):
    n, hd = q.shape
    heads = hd // LANES
    nq = seq // tq
    kv_spec = pl.BlockSpec((seq, hpb * LANES), lambda b, h, i: (b, h))
    q_spec = pl.BlockSpec((tq, hpb * LANES), lambda b, h, i: (b * nq + i, h))
    return pl.pallas_call(
        functools.partial(_attn_prefill_kernel, tq=tq, lam_init=lam_init, hpb=hpb),
        out_shape=jax.ShapeDtypeStruct((n, hd), BF16),
        grid=(batch, heads // hpb, nq),
        in_specs=[pl.BlockSpec(lam_vec.shape, lambda b, h, i: (0, 0)), q_spec, kv_spec, kv_spec,
                  pl.BlockSpec((1, LANES), lambda b, h, i: (0, 0))],
        out_specs=q_spec,
        compiler_params=_params("parallel", "parallel", "arbitrary"),
        name="attn_prefill",
    )(lam_vec, q, k, v, subln_w.reshape(1, LANES))


def _attn_decode_kernel(pt_ref, lam_ref, q_ref, kn_ref, vn_ref, *rest, lam_init, pg):
    kc_refs, vc_refs = rest[:pg], rest[pg:2 * pg]
    w_ref, o_ref, m_ref, l_ref, acc_ref = rest[2 * pg:]
    p = pl.program_id(1)
    q = q_ref[0]
    heads = q.shape[0]
    lane = lax.broadcasted_iota(I32, (1, LANES), 1)
    lo = lane < DIFF_HEAD_DIM
    q2 = jnp.concatenate([jnp.where(lo, q, 0.0), jnp.where(lo, 0.0, q)], axis=0)

    @pl.when(p == 0)
    def _():
        kn2 = jnp.concatenate([kn_ref[0], kn_ref[0]], axis=0)
        m_ref[...] = jnp.sum(q2 * kn2, axis=-1, keepdims=True)
        l_ref[...] = jnp.ones_like(l_ref)
        acc_ref[...] = jnp.concatenate([vn_ref[0], vn_ref[0]], axis=0)

    n_rows = kc_refs[0].shape[1] * heads
    own = (lax.broadcasted_iota(I32, (2 * heads, n_rows), 1) % heads
           == lax.broadcasted_iota(I32, (2 * heads, n_rows), 0) % heads)
    q16 = q2.astype(BF16)
    s = [jnp.where(own, _dot_nt(q16, kc_refs[j][0].reshape(n_rows, LANES).astype(BF16)), NEG_BIG)
         for j in range(pg)]
    m_old = m_ref[...]
    m_new = m_old
    for j in range(pg):
        m_new = jnp.maximum(m_new, jnp.max(s[j], axis=1, keepdims=True))
    alpha = jnp.exp(m_old - m_new)
    l_new = alpha * l_ref[...]
    acc = alpha * acc_ref[...]
    for j in range(pg):
        pr = jnp.exp(s[j] - m_new)
        l_new = l_new + jnp.sum(pr, axis=1, keepdims=True)
        acc = acc + _dot(pr.astype(BF16), vc_refs[j][0].reshape(n_rows, LANES).astype(BF16))
    l_ref[...] = l_new
    acc_ref[...] = acc
    m_ref[...] = m_new

    @pl.when(p == pl.num_programs(1) - 1)
    def _():
        lam = _lambda_value(lam_ref, lam_init)
        o = (acc_ref[0:heads, :] / l_ref[0:heads, :]
             - lam * (acc_ref[heads:2 * heads, :] / l_ref[heads:2 * heads, :]))
        o = o * lax.rsqrt(jnp.mean(o * o, axis=-1, keepdims=True) + RMS_EPS) * w_ref[...] * (1.0 - lam_init)
        o_ref[0] = o.astype(o_ref.dtype)


def attn_decode(q, k_new, v_new, cache_k, cache_v, page_table, lam_vec, subln_w, *, lam_init):
    b, heads, _ = q.shape
    n_pages = page_table.shape[1]
    page = cache_k.shape[1]
    pg = math.gcd(n_pages, DECODE_PAGES_PER_STEP)
    tok = pl.BlockSpec((1, heads, LANES), lambda i, p, pt: (i, 0, 0))

    def cache(j):
        return pl.BlockSpec((1, page, heads, LANES), lambda i, p, pt: (pt[i * n_pages + p * pg + j], 0, 0, 0))

    grid_spec = pltpu.PrefetchScalarGridSpec(
        num_scalar_prefetch=1,
        grid=(b, n_pages // pg),
        in_specs=[pl.BlockSpec(lam_vec.shape, lambda i, p, pt: (0, 0)), tok, tok, tok]
        + [cache(j) for j in range(pg)] * 2
        + [pl.BlockSpec((1, LANES), lambda i, p, pt: (0, 0))],
        out_specs=tok,
        scratch_shapes=[pltpu.VMEM((2 * heads, 1), F32), pltpu.VMEM((2 * heads, 1), F32),
                        pltpu.VMEM((2 * heads, LANES), F32)],
    )
    return pl.pallas_call(
        functools.partial(_attn_decode_kernel, lam_init=lam_init, pg=pg),
        out_shape=jax.ShapeDtypeStruct(q.shape, BF16),
        grid_spec=grid_spec,
        compiler_params=_params("parallel", "arbitrary"),
        name="attn_decode",
    )(page_table.reshape(-1), lam_vec, q, k_new, v_new, *([cache_k] * pg), *([cache_v] * pg),
      subln_w.reshape(1, LANES))


def _router_kernel(x_ref, sc_ref, sh_ref, wr_ref, br_ref, idx_ref, gate_ref, pos_ref, cnt_ref,
                   tri_ref, carry_ref, *, tm):
    i = pl.program_id(0)

    @pl.when(i == 0)
    def _():
        r = lax.broadcasted_iota(I32, (tm, tm), 0)
        c = lax.broadcasted_iota(I32, (tm, tm), 1)
        tri_ref[...] = jnp.where(c < r, 1.0, 0.0).astype(BF16)
        carry_ref[...] = jnp.zeros_like(carry_ref)

    h = x_ref[...] * (1.0 + sc_ref[0]) + sh_ref[0]
    h_hi = h.astype(BF16)
    h_lo = (h - h_hi.astype(F32)).astype(BF16)
    w = wr_ref[...]
    w_hi = w.astype(BF16)
    w_lo = (w - w_hi.astype(F32)).astype(BF16)
    logits = _dot(h_hi, w_hi) + _dot(h_lo, w_hi) + _dot(h_hi, w_lo) + br_ref[...]
    lane = lax.broadcasted_iota(I32, (1, LANES), 1)
    work = jnp.where(lane < N_EXPERTS, logits, NEG_BIG)
    tops, idxs = [], []
    member = jnp.zeros((tm, LANES), F32)
    for _ in range(TOP_K):
        mx = jnp.max(work, axis=1, keepdims=True)
        ix = jnp.min(jnp.where(work == mx, lane, LANES), axis=1, keepdims=True)
        hit = lane == ix
        member = jnp.where(hit, 1.0, member)
        work = jnp.where(hit, NEG_BIG, work)
        tops.append(mx)
        idxs.append(ix)
    ex = [jnp.exp(t - tops[0]) for t in tops]
    denom = ex[0] + ex[1] + ex[2] + ex[3]
    rank = _dot(tri_ref[...], member.astype(BF16)) + carry_ref[...]
    idx_out = jnp.zeros((tm, LANES), I32)
    gate_out = jnp.zeros((tm, LANES), F32)
    pos_out = jnp.zeros((tm, LANES), I32)
    for k in range(TOP_K):
        pk = jnp.sum(jnp.where(lane == idxs[k], rank, 0.0), axis=1, keepdims=True)
        idx_out = jnp.where(lane == k, idxs[k], idx_out)
        gate_out = jnp.where(lane == k, ex[k] / denom, gate_out)
        pos_out = jnp.where(lane == k, pk.astype(I32), pos_out)
    idx_ref[...] = idx_out
    gate_ref[...] = gate_out
    pos_ref[...] = pos_out
    carry_ref[...] = carry_ref[...] + jnp.sum(member, axis=0, keepdims=True)
    cnt_ref[...] = carry_ref[...].astype(I32)


def router(x, sc, sh, w_r, b_r, *, rows_per_group, tm):
    n, d = x.shape
    tiles_per_group = rows_per_group // tm
    r = sc.shape[1]
    w_pad = jnp.zeros((d, LANES), F32).at[:, :N_EXPERTS].set(w_r)
    b_pad = jnp.zeros((1, LANES), F32).at[0, :N_EXPERTS].set(b_r)
    mod = pl.BlockSpec((1, r, d), lambda i: (i // tiles_per_group, 0, 0))
    wide = pl.BlockSpec((tm, LANES), lambda i: (i, 0))
    return pl.pallas_call(
        functools.partial(_router_kernel, tm=tm),
        out_shape=(jax.ShapeDtypeStruct((n, LANES), I32), jax.ShapeDtypeStruct((n, LANES), F32),
                   jax.ShapeDtypeStruct((n, LANES), I32), jax.ShapeDtypeStruct((1, LANES), I32)),
        grid=(n // tm,),
        in_specs=[pl.BlockSpec((tm, d), lambda i: (i, 0)), mod, mod,
                  pl.BlockSpec((d, LANES), lambda i: (0, 0)), pl.BlockSpec((1, LANES), lambda i: (0, 0))],
        out_specs=(wide, wide, wide, pl.BlockSpec((1, LANES), lambda i: (0, 0))),
        scratch_shapes=[pltpu.VMEM((tm, tm), BF16), pltpu.VMEM((1, LANES), F32)],
        compiler_params=_params("arbitrary"),
        name="moe_router",
    )(x, sc, sh, w_pad, b_pad)


def _dispatch_kernel(gap_lo_ref, gap_hi_ref, x_ref, sc_ref, sh_ref, dest_ref, xs_ref,
                     h_ref, zero_ref, idx_ref, sem_ref, isem_ref, *, tm, n_gap_rows):
    i = pl.program_id(0)
    n_idx = tm * TOP_K
    idx_copy = pltpu.make_async_copy(dest_ref.at[0, 0], idx_ref, isem_ref)
    idx_copy.start()
    h = x_ref[...] * (1.0 + sc_ref[0]) + sh_ref[0]
    lg = h.shape[1] // LANES
    for g in range(lg):
        h_ref[pl.ds(g, tm, stride=lg), :] = h[:, g * LANES:(g + 1) * LANES]
    idx_copy.wait()

    def row_copy(src, src_tok, dst_tok):
        return pltpu.make_async_copy(_token_rows(src, src_tok, lg), _token_rows(xs_ref, dst_tok, lg), sem_ref)

    def issue(r, carry):
        for k in range(TOP_K):
            row_copy(h_ref, r, idx_ref[r * TOP_K + k]).start()
        return carry

    lax.fori_loop(0, tm, issue, 0, unroll=DMA_UNROLL)

    def drain(r, carry):
        row_copy(h_ref, 0, 0).wait()
        return carry

    lax.fori_loop(0, n_idx, drain, 0, unroll=DMA_UNROLL)

    @pl.when(i == 0)
    def _():
        zero_ref[...] = jnp.zeros_like(zero_ref)

        def per_expert(e, carry):
            def fill(rw, c2):
                row_copy(zero_ref, 0, rw).start()
                return c2
            return lax.fori_loop(gap_lo_ref[e], gap_hi_ref[e], fill, carry)

        lax.fori_loop(0, N_EXPERTS, per_expert, 0)

        def drain_zero(r, carry):
            row_copy(zero_ref, 0, 0).wait()
            return carry

        lax.fori_loop(0, n_gap_rows, drain_zero, 0)


def dispatch(x, sc, sh, dest, gap_lo, gap_hi, *, rows_per_group, tm, n_slots_padded):
    n, d = x.shape
    tiles_per_group = rows_per_group // tm
    r = sc.shape[1]
    n_idx = tm * TOP_K
    mod = pl.BlockSpec((1, r, d), lambda i, lo, hi: (i // tiles_per_group, 0, 0))
    grid_spec = pltpu.PrefetchScalarGridSpec(
        num_scalar_prefetch=2,
        grid=(n // tm,),
        in_specs=[pl.BlockSpec((tm, d), lambda i, lo, hi: (i, 0)), mod, mod,
                  pl.BlockSpec((1, 1, n_idx), lambda i, lo, hi: (i, 0, 0))],
        out_specs=pl.BlockSpec(memory_space=pl.ANY),
        scratch_shapes=[pltpu.VMEM((tm * (d // LANES), LANES), F32), pltpu.VMEM((d // LANES, LANES), F32),
                        pltpu.SMEM((n_idx,), I32), pltpu.SemaphoreType.DMA, pltpu.SemaphoreType.DMA],
    )
    return pl.pallas_call(
        functools.partial(_dispatch_kernel, tm=tm, n_gap_rows=n_slots_padded - n * TOP_K),
        out_shape=jax.ShapeDtypeStruct((n_slots_padded * (d // LANES), LANES), F32),
        grid_spec=grid_spec,
        compiler_params=_params("arbitrary"),
        name="moe_dispatch",
    )(gap_lo, gap_hi, x, sc, sh, dest.reshape(n // tm, 1, n_idx))


def _expert_ffn_kernel(be_ref, nu_ref, xs_ref, wgu_ref, bgu_ref, wd_ref, bd_ref, ys_ref, wgu16_ref, wd16_ref,
                       *, blk):
    i = pl.program_id(0)
    active = i < nu_ref[0]

    @pl.when(jnp.logical_not(active))
    def _():
        ys_ref[...] = jnp.zeros_like(ys_ref)

    @pl.when(active & ((i == 0) | (be_ref[i] != be_ref[jnp.maximum(i - 1, 0)])))
    def _():
        half = GU_GROUP // 2
        r = lax.broadcasted_iota(I32, (GU_GROUP, GU_GROUP), 0)
        c = lax.broadcasted_iota(I32, (GU_GROUP, GU_GROUP), 1)
        perm = jnp.where(r == jnp.where(c < half, 2 * c, 2 * (c - half) + 1), 1.0, 0.0).astype(BF16)
        for g in range(wgu_ref.shape[2] // GU_GROUP):
            cols = slice(g * GU_GROUP, (g + 1) * GU_GROUP)
            wgu16_ref[:, cols] = _dot(wgu_ref[0, :, cols].astype(BF16), perm).astype(BF16)
        wd16_ref[...] = wd_ref[0].astype(BF16)

    @pl.when(active)
    def _():
        lg = xs_ref.shape[0] // blk
        x = jnp.concatenate([xs_ref[pl.ds(g, blk, stride=lg), :] for g in range(lg)], axis=1).astype(BF16)
        gu = _dot(x, wgu16_ref[...]) + bgu_ref[0]
        half = GU_GROUP // 2
        acts = []
        for g in range(gu.shape[1] // GU_GROUP):
            gate = jnp.minimum(gu[:, g * GU_GROUP:g * GU_GROUP + half], SWIGLU_LIMIT)
            up = jnp.clip(gu[:, g * GU_GROUP + half:(g + 1) * GU_GROUP], -SWIGLU_LIMIT, SWIGLU_LIMIT)
            acts.append((up + 1.0) * (gate * _sigmoid(gate * SWIGLU_ALPHA)))
        act = jnp.concatenate(acts, axis=1).astype(BF16)
        y = _dot(act, wd16_ref[...]) + bd_ref[0]
        for g in range(lg):
            ys_ref[pl.ds(g, blk, stride=lg), :] = y[:, g * LANES:(g + 1) * LANES]


def expert_ffn(xs, block_expert, n_used_blocks, wgu, bgu, wd, bd, *, blk, expert_base):
    d, de2 = wgu.shape[1], wgu.shape[2]
    lg = d // LANES
    ns = xs.shape[0] // lg
    rows_in = pl.BlockSpec((blk * lg, LANES), lambda i, be, nu: (jnp.minimum(i, nu[0] - 1), 0))
    rows_out = pl.BlockSpec((blk * lg, LANES), lambda i, be, nu: (i, 0))
    wspec = lambda a, b: pl.BlockSpec((1, a, b), lambda i, be, nu: (expert_base + be[i], 0, 0))
    grid_spec = pltpu.PrefetchScalarGridSpec(
        num_scalar_prefetch=2,
        grid=(ns // blk,),
        in_specs=[rows_in, wspec(d, de2), wspec(1, de2), wspec(de2 // 2, d), wspec(1, d)],
        out_specs=rows_out,
        scratch_shapes=[pltpu.VMEM((d, de2), BF16), pltpu.VMEM((de2 // 2, d), BF16)],
    )
    return pl.pallas_call(
        functools.partial(_expert_ffn_kernel, blk=blk),
        out_shape=jax.ShapeDtypeStruct(xs.shape, F32),
        grid_spec=grid_spec,
        compiler_params=_params("arbitrary"),
        name="moe_expert_ffn",
    )(block_expert, n_used_blocks, xs, wgu, bgu, wd, bd)


def _combine_ln_kernel(x_ref, gm_ref, gate_ref, dest_ref, ys_ref, lg_ref, lb_ref, o_ref,
                       buf_ref, idx_ref, sem_ref, isem_ref, *, tm, alpha):
    n_idx = tm * TOP_K
    idx_copy = pltpu.make_async_copy(dest_ref.at[0, 0], idx_ref, isem_ref)
    idx_copy.start()
    idx_copy.wait()

    lg = buf_ref.shape[1] // tm

    def row_copy(src_tok, k, r):
        return pltpu.make_async_copy(_token_rows(ys_ref, src_tok, lg), _token_rows(buf_ref.at[k], r, lg), sem_ref)

    def issue(r, carry):
        for k in range(TOP_K):
            row_copy(idx_ref[r * TOP_K + k], k, r).start()
        return carry

    lax.fori_loop(0, tm, issue, 0, unroll=DMA_UNROLL)

    def drain(r, carry):
        row_copy(0, 0, 0).wait()
        return carry

    lax.fori_loop(0, n_idx, drain, 0, unroll=DMA_UNROLL)
    gates = gate_ref[...]
    cols = []
    for g in range(lg):
        yg = gates[:, 0:1] * buf_ref[0, pl.ds(g, tm, stride=lg), :]
        for k in range(1, TOP_K):
            yg = yg + gates[:, k:k + 1] * buf_ref[k, pl.ds(g, tm, stride=lg), :]
        cols.append(yg)
    y = jnp.concatenate(cols, axis=1)
    o_ref[...] = _layer_norm(alpha * x_ref[...] + gm_ref[0] * y, lg_ref[...], lb_ref[...])


def combine_ln(x, gm, gates, dest, ys, ln_g, ln_b, *, rows_per_group, tm, alpha):
    n, d = x.shape
    tiles_per_group = rows_per_group // tm
    r = gm.shape[1]
    n_idx = tm * TOP_K
    row = pl.BlockSpec((tm, d), lambda i: (i, 0))
    vec = pl.BlockSpec((1, d), lambda i: (0, 0))
    return pl.pallas_call(
        functools.partial(_combine_ln_kernel, tm=tm, alpha=alpha),
        out_shape=jax.ShapeDtypeStruct((n, d), F32),
        grid=(n // tm,),
        in_specs=[row, pl.BlockSpec((1, r, d), lambda i: (i // tiles_per_group, 0, 0)),
                  pl.BlockSpec((tm, LANES), lambda i: (i, 0)),
                  pl.BlockSpec((1, 1, n_idx), lambda i: (i, 0, 0)),
                  pl.BlockSpec(memory_space=pl.ANY), vec, vec],
        out_specs=row,
        scratch_shapes=[pltpu.VMEM((TOP_K, tm * (d // LANES), LANES), F32), pltpu.SMEM((n_idx,), I32),
                        pltpu.SemaphoreType.DMA, pltpu.SemaphoreType.DMA],
        compiler_params=_params("arbitrary"),
        name="moe_combine_ln",
    )(x, gm, gates, dest.reshape(n // tm, 1, n_idx), ys, ln_g.reshape(1, d), ln_b.reshape(1, d))


def _moe_layer(x, sc, sh, gm, ln_g, ln_b, w_r, b_r, wgu, bgu, wd, bd, *, expert_base, rows_per_group, tm, blk,
               alpha):
    n, d = x.shape
    idx_w, gate_w, pos_w, cnt_w = router(x, sc, sh, w_r, b_r, rows_per_group=rows_per_group, tm=tm)
    counts = cnt_w[0, :N_EXPERTS]
    top_idx, pos = idx_w[:, :TOP_K], pos_w[:, :TOP_K]
    n_slots = n * TOP_K
    n_blocks = -(-(n_slots + N_EXPERTS * (blk - 1)) // blk)
    padded = (counts + blk - 1) // blk * blk
    pad_end = jnp.cumsum(padded)
    pad_start = pad_end - padded
    onehot = top_idx[..., None] == jnp.arange(N_EXPERTS, dtype=I32)
    dest = (jnp.sum(jnp.where(onehot, pad_start, 0), axis=-1) + pos).astype(I32)
    block_start = jnp.arange(n_blocks, dtype=I32) * blk
    block_expert = jnp.minimum(
        jnp.sum((pad_end[None, :] <= block_start[:, None]).astype(I32), axis=1), N_EXPERTS - 1).astype(I32)
    gap_lo = (pad_start + counts).astype(I32)
    gap_hi = jnp.concatenate([pad_start[1:], jnp.full((1,), n_blocks * blk, I32)]).astype(I32)
    xs = dispatch(x, sc, sh, dest, gap_lo, gap_hi, rows_per_group=rows_per_group, tm=tm,
                  n_slots_padded=n_blocks * blk)
    n_used_blocks = (pad_end[-1:] // blk).astype(I32)
    ys = expert_ffn(xs, block_expert, n_used_blocks, wgu, bgu, wd, bd, blk=blk, expert_base=expert_base)
    return combine_ln(x, gm, gate_w, dest, ys, ln_g, ln_b, rows_per_group=rows_per_group, tm=tm, alpha=alpha)


def _rope_tables(pos):
    inv_freq = ROPE_THETA ** (-jnp.arange(ROPE_HALF, dtype=F32) / ROPE_HALF)
    ang = pos.astype(F32)[:, None] * inv_freq[None, :]
    cos, sin = jnp.cos(ang), jnp.sin(ang)
    reps = LANES // DIFF_HEAD_DIM
    return (jnp.concatenate([cos, cos] * reps, axis=1), jnp.concatenate([-sin, sin] * reps, axis=1))


def _trunk(x, mods, kv_mods, past_len, s0, cache, wts, *, batch, seq, tm, blk):
    n, d = x.shape
    depth = wts["ada_w"].shape[0]
    n_a = wts["hgrn_w_in"].shape[0]
    alpha = (2 * depth) ** 0.25
    rows_per_group = seq if seq >= tm else n
    kw = dict(rows_per_group=rows_per_group, tm=tm)
    decode = seq == 1
    pos = past_len + (jnp.zeros((n,), I32) if decode else jnp.arange(seq, dtype=I32))
    rope = _rope_tables(pos)
    heads = d // LANES
    states = []
    k_new = v_new = k16 = v16 = None
    for l in range(depth):
        sh_m, sc_m, g_m, sh_f, sc_f, g_f = mods[l]
        if l < n_a:
            proj, = modmm(x, sc_m, sh_m, wts["hgrn_w_in"][l], out_dtypes=(F32,), **kw)
            lbs = [a[l][None] for a in wts["lb"]]
            nw = wts["hgrn_norm_w"][l].reshape(1, LANES)
            if decode:
                o, s_l = hgrn_step(proj, *lbs, nw, s0[l])
            else:
                o, s_l = hgrn_chunked(proj, *lbs, nw, s0[l], batch=batch, seq=seq, tb=min(seq, 512))
            states.append(s_l)
            w_out = wts["hgrn_w_out"][l]
        else:
            j = l - n_a
            lam_init = 0.8 - 0.6 * math.exp(-0.3 * l)
            if l == n_a:
                kv_shift, kv_scale = kv_mods
                half = wts["kv_w"].shape[1] // 2
                k_new, k16 = modmm(x, kv_scale, kv_shift, wts["kv_w"][:, :half], out_dtypes=(F32, BF16),
                                   rope=rope, **kw)
                v_new, v16 = modmm(x, kv_scale, kv_shift, wts["kv_w"][:, half:], out_dtypes=(F32, BF16), **kw)
            q_scale = DIFF_HEAD_DIM ** -0.5
            if decode:
                q, = modmm(x, sc_m, sh_m, wts["diff_w_q"][j], out_dtypes=(F32,), rope=rope, out_scale=q_scale, **kw)
                o = attn_decode(q.reshape(n, heads, LANES), k_new.reshape(n, heads, LANES),
                                v_new.reshape(n, heads, LANES), cache[0], cache[1], cache[2],
                                wts["diff_lambda"][j], wts["diff_subln_w"][j], lam_init=lam_init)
                o = o.reshape(n, d)
            else:
                q, = modmm(x, sc_m, sh_m, wts["diff_w_q"][j], out_dtypes=(BF16,), rope=rope, out_scale=q_scale, **kw)
                o = attn_prefill(q, k16, v16, wts["diff_lambda"][j], wts["diff_subln_w"][j],
                                 batch=batch, seq=seq, tq=min(seq, 1024), lam_init=lam_init, hpb=1)
            w_out = wts["diff_w_out"][j]
        x = proj_ln(o, w_out, x, g_m, wts["ln_g"][l, 0], wts["ln_b"][l, 0], alpha=alpha, **kw)
        x = _moe_layer(x, sc_f, sh_f, g_f, wts["ln_g"][l, 1], wts["ln_b"][l, 1],
                       wts["moe_w_router"][l], wts["moe_b_router"][l],
                       wts["moe_w_gu"], wts["moe_b_gu"], wts["moe_w_down"], wts["moe_b_down"],
                       expert_base=l * N_EXPERTS, alpha=alpha, blk=blk, **kw)
    return x, k_new, v_new, jnp.stack(states)


def kernel(x_prompt, x_sample, cache_k, cache_v, state_hgrn, page_table, c_prompt, c_sample, ada_w, ada_b, ln_g, ln_b, hgrn_w_in, hgrn_lb_logits, hgrn_norm_w, hgrn_w_out, kv_ada_w, kv_ada_b, kv_w, diff_w_q, diff_lambda, diff_subln_w, diff_w_out, moe_w_router, moe_b_router, moe_w_gu, moe_b_gu, moe_w_down, moe_b_down):
    b_p, t_p, d = x_prompt.shape
    b_d, t_d, _ = x_sample.shape
    depth = ada_w.shape[0]
    n_a = hgrn_w_in.shape[0]
    heads = d // LANES
    de = moe_w_down.shape[2]

    m_rows = -(-(b_d + b_p) // 8) * 8
    c_all = jnp.zeros((m_rows, d), F32).at[:b_d].set(c_sample).at[b_d:b_d + b_p].set(c_prompt)
    mod_all = adaln(c_all, ada_w, ada_b)
    kv_mod_all = adaln(c_all, kv_ada_w[None], kv_ada_b[None])[0]

    def split_mods(m, n_vec, lo, hi, per_row):
        parts = [m[lo:hi, v * d:(v + 1) * d] for v in range(n_vec)]
        return [p[None] if per_row else p[:, None, :] for p in parts]

    mods_p = [split_mods(mod_all[l], 6, b_d, b_d + b_p, False) for l in range(depth)]
    mods_s = [split_mods(mod_all[l], 6, 0, b_d, True) for l in range(depth)]
    kv_mods_p = split_mods(kv_mod_all, 2, b_d, b_d + b_p, False)
    kv_mods_s = split_mods(kv_mod_all, 2, 0, b_d, True)

    loglb, log1m, onem = hgrn_lower_bounds(hgrn_lb_logits)

    n_exp = moe_w_gu.shape[1]
    half = GU_GROUP // 2
    b_gu = moe_b_gu.reshape(depth, n_exp, 2 * de // GU_GROUP, half, 2)
    b_gu = jnp.swapaxes(b_gu, -1, -2).reshape(depth * n_exp, 1, 2 * de)

    wts = dict(
        ada_w=ada_w, ln_g=ln_g, ln_b=ln_b,
        hgrn_w_in=hgrn_w_in.astype(BF16), lb=(loglb, log1m, onem), hgrn_norm_w=hgrn_norm_w,
        hgrn_w_out=hgrn_w_out.astype(BF16), kv_w=kv_w.astype(BF16), diff_w_q=diff_w_q.astype(BF16),
        diff_lambda=diff_lambda, diff_subln_w=diff_subln_w, diff_w_out=diff_w_out.astype(BF16),
        moe_w_router=moe_w_router, moe_b_router=moe_b_router,
        moe_w_gu=moe_w_gu.reshape(depth * n_exp, d, 2 * de), moe_b_gu=b_gu,
        moe_w_down=moe_w_down.reshape(depth * n_exp, de, d), moe_b_down=moe_b_down.reshape(depth * n_exp, 1, d),
    )

    s0_prompt = jnp.zeros((n_a, b_p, heads, LANES, LANES), F32)
    y_p, k_p, v_p, st_p = _trunk(x_prompt.reshape(b_p * t_p, d), mods_p, kv_mods_p, 0, s0_prompt, None, wts,
                                 batch=b_p, seq=t_p, tm=min(512, t_p), blk=256)
    past_len = page_table.shape[1] * cache_k.shape[1]
    y_s, k_s, v_s, st_s = _trunk(x_sample.reshape(b_d * t_d, d), mods_s, kv_mods_s, past_len, state_hgrn,
                                 (cache_k, cache_v, page_table), wts, batch=b_d, seq=t_d, tm=b_d, blk=16)
    dv = d // heads
    return (y_p.reshape(b_p, t_p, d), y_s.reshape(b_d, t_d, d),
            k_p.reshape(b_p, t_p, heads, dv), v_p.reshape(b_p, t_p, heads, dv), st_p,
            k_s.reshape(b_d, t_d, heads, dv), v_s.reshape(b_d, t_d, heads, dv), st_s)
```

```python
import functools
import math

import jax
import jax.numpy as jnp
from jax import lax
from jax.experimental import pallas as pl
from jax.experimental.pallas import tpu as pltpu

F32 = jnp.float32
BF16 = jnp.bfloat16
I32 = jnp.int32

LANES = 128
HGRN_CHUNK = 64
HGRN_SUB = 16
LB_FLOOR = 1e-30
LOG2E = math.log2(math.e)
DIFF_HEAD_DIM = 64
ROPE_THETA = 10000.0
ROPE_HALF = DIFF_HEAD_DIM // 2
NEG_BIG = -1e30
N_EXPERTS = 32
TOP_K = 4
SWIGLU_LIMIT = 7.0
SWIGLU_ALPHA = 1.702
LN_EPS = 1e-5
RMS_EPS = 1e-5
VMEM_LIMIT = 56 * 1024 * 1024
DECODE_PAGES_PER_STEP = 8
GU_GROUP = 256
DMA_UNROLL = 8
DMA_PRIORITIES = 2


def _params(*sem):
    return pltpu.CompilerParams(dimension_semantics=sem, vmem_limit_bytes=VMEM_LIMIT)


def _sigmoid(x):
    return 1.0 / (1.0 + jnp.exp(-x))


def _silu(x):
    return x * _sigmoid(x)


def _log_sigmoid(x):
    return jnp.minimum(x, 0.0) - jnp.log1p(jnp.exp(-jnp.abs(x)))


def _layer_norm(z, g, b):
    mu = jnp.mean(z, axis=-1, keepdims=True)
    zc = z - mu
    var = jnp.mean(zc * zc, axis=-1, keepdims=True)
    return zc * lax.rsqrt(var + LN_EPS) * g + b


def _dot(a, b):
    return jnp.dot(a, b, preferred_element_type=F32)


def _dot_nt(a, b):
    return lax.dot_general(a, b, (((1,), (1,)), ((), ())), preferred_element_type=F32)


def _dot_tn(a, b):
    return lax.dot_general(a, b, (((0,), (0,)), ((), ())), preferred_element_type=F32)


def _token_rows(ref, tok, lg):
    start = tok * lg
    return ref.at[pl.ds(start if isinstance(start, int) else pl.multiple_of(start, lg), lg)]


def _split3(x):
    hi = x.astype(BF16)
    r1 = x - hi.astype(F32)
    mid = r1.astype(BF16)
    lo = (r1 - mid.astype(F32)).astype(BF16)
    return hi, mid, lo


def _adaln_kernel(c_ref, w_ref, b_ref, o_ref):
    s = _silu(c_ref[...]).astype(BF16)
    o_ref[0] = _dot(s, w_ref[0].astype(BF16)) + b_ref[0]


def adaln(c, w, b, tn=1024):
    m, d = c.shape
    nl, _, no = w.shape
    return pl.pallas_call(
        _adaln_kernel,
        out_shape=jax.ShapeDtypeStruct((nl, m, no), F32),
        grid=(nl, no // tn),
        in_specs=[
            pl.BlockSpec((m, d), lambda l, j: (0, 0)),
            pl.BlockSpec((1, d, tn), lambda l, j: (l, 0, j)),
            pl.BlockSpec((1, 1, tn), lambda l, j: (l, 0, j)),
        ],
        out_specs=pl.BlockSpec((1, m, tn), lambda l, j: (l, 0, j)),
        compiler_params=_params("parallel", "parallel"),
        name="adaln",
    )(c, w, b.reshape(nl, 1, no))


def _lb_kernel(logit_ref, loglb_ref, log1m_ref, onem_ref):
    x = logit_ref[...]
    nl = x.shape[0]
    e = jnp.exp(x - jnp.max(x, axis=0, keepdims=True))
    sm = e / jnp.sum(e, axis=0, keepdims=True)
    acc = jnp.zeros_like(sm[0:1])
    for l in range(nl):
        lb = jnp.clip(acc, 0.0, 1.0 - 1e-6)
        loglb_ref[l] = jnp.log(jnp.maximum(lb, LB_FLOOR))
        log1m_ref[l] = jnp.log1p(-lb)
        onem_ref[l] = 1.0 - lb
        acc = acc + sm[l:l + 1]


def hgrn_lower_bounds(lb_logits):
    nl, f = lb_logits.shape
    shp = jax.ShapeDtypeStruct((nl, 1, f), F32)
    return pl.pallas_call(_lb_kernel, out_shape=(shp, shp, shp), name="hgrn_lb")(lb_logits)


def _rotary_tile(y, cos, sin_signed):
    lane = lax.broadcasted_iota(I32, (1, LANES), 1)
    first_half = (lane % DIFF_HEAD_DIM) < ROPE_HALF
    partner = jnp.where(first_half, pltpu.roll(y, LANES - ROPE_HALF, 1), pltpu.roll(y, ROPE_HALF, 1))
    return y * cos + partner * sin_signed


def _modmm_kernel(*refs, rotary, out_scale, n_out):
    if rotary:
        x_ref, sc_ref, sh_ref, w_ref, cos_ref, sin_ref = refs[:6]
        rest = refs[6:]
    else:
        x_ref, sc_ref, sh_ref, w_ref = refs[:4]
        rest = refs[4:]
    out_refs, h_ref = rest[:n_out], rest[n_out]

    @pl.when(pl.program_id(1) == 0)
    def _():
        h_ref[...] = (x_ref[...] * (1.0 + sc_ref[0]) + sh_ref[0]).astype(BF16)

    y = _dot(h_ref[...], w_ref[...])
    if rotary:
        cos, sin = cos_ref[...], sin_ref[...]
        y = jnp.concatenate(
            [_rotary_tile(y[:, g * LANES:(g + 1) * LANES], cos, sin) for g in range(y.shape[1] // LANES)], axis=1)
    if out_scale != 1.0:
        y = y * out_scale
    for o_ref in out_refs:
        o_ref[...] = y.astype(o_ref.dtype)


def modmm(x, sc, sh, w, *, rows_per_group, tm, out_dtypes, rope=None, out_scale=1.0, tn=1024):
    n, d = x.shape
    no = w.shape[1]
    tiles_per_group = rows_per_group // tm
    r = sc.shape[1]
    in_specs = [
        pl.BlockSpec((tm, d), lambda i, j: (i, 0)),
        pl.BlockSpec((1, r, d), lambda i, j: (i // tiles_per_group, 0, 0)),
        pl.BlockSpec((1, r, d), lambda i, j: (i // tiles_per_group, 0, 0)),
        pl.BlockSpec((d, tn), lambda i, j: (0, j)),
    ]
    args = [x, sc, sh, w]
    if rope is not None:
        pos_tiles = rope[0].shape[0] // tm
        in_specs += [pl.BlockSpec((tm, LANES), lambda i, j: (i % pos_tiles, 0))] * 2
        args += list(rope)
    outs = tuple(jax.ShapeDtypeStruct((n, no), dt) for dt in out_dtypes)
    res = pl.pallas_call(
        functools.partial(_modmm_kernel, rotary=rope is not None, out_scale=out_scale, n_out=len(outs)),
        out_shape=outs,
        grid=(n // tm, no // tn),
        in_specs=in_specs,
        out_specs=tuple(pl.BlockSpec((tm, tn), lambda i, j: (i, j)) for _ in outs),
        scratch_shapes=[pltpu.VMEM((tm, d), BF16)],
        compiler_params=_params("parallel", "arbitrary"),
        name="modmm",
    )(*args)
    return res


def _hgrn_head_chunk(q_raw, f, v, g_raw, loglb, log1m, onem, nw, st, tri, sub_row):
    c, sub = HGRN_CHUNK, HGRN_SUB
    half = sub // 2
    q = _silu(q_raw)
    x2 = log1m + _log_sigmoid(f)
    mx = jnp.maximum(loglb, x2)
    logf = mx + jnp.log1p(jnp.exp(jnp.minimum(loglb, x2) - mx))
    kk = onem * _sigmoid(-f)
    hi, mid, lo = _split3(logf)
    b = (_dot(tri, hi) + _dot(tri, mid) + _dot(tri, lo)) * LOG2E
    v16 = v.astype(BF16)
    o_inter = _dot_nt((q * jnp.exp2(b)).astype(BF16), st.astype(BF16))
    o_parts = []
    for i in range(c // sub):
        sl = slice(i * sub, (i + 1) * sub)
        qi, bi, ki, vi = q[sl], b[sl], kk[sl], v[sl]
        oi = o_inter[sl]
        top, bot = oi[:half], oi[half:]
        for s in range(sub):
            k_s, b_s, v_s = ki[s:s + 1], bi[s:s + 1], vi[s:s + 1]
            a_bot = jnp.sum(qi[half:] * k_s * jnp.exp2(bi[half:] - b_s), axis=1, keepdims=True)
            if s < half:
                a_top = jnp.sum(qi[:half] * k_s * jnp.exp2(bi[:half] - b_s), axis=1, keepdims=True)
                top = top + jnp.where(sub_row >= s, a_top, 0.0) * v_s
            else:
                a_bot = jnp.where(sub_row >= s - half, a_bot, 0.0)
            bot = bot + a_bot * v_s
        oi = jnp.concatenate([top, bot], axis=0)
        if i > 0:
            r = b[i * sub - 1:i * sub]
            qs = (qi * jnp.exp2(bi - r)).astype(BF16)
            ks = (kk[:i * sub] * jnp.exp2(r - b[:i * sub])).astype(BF16)
            a = _dot_nt(qs, ks)
            oi = oi + _dot(a.astype(BF16), v16[:i * sub])
        o_parts.append(oi)
    o = jnp.concatenate(o_parts, axis=0)
    b_last = b[c - 1:c]
    kd = (kk * jnp.exp2(b_last - b)).astype(BF16)
    st_new = st * jnp.exp2(b_last) + _dot_tn(v16, kd)
    o = o * lax.rsqrt(jnp.mean(o * o, axis=-1, keepdims=True) + RMS_EPS) * nw * _silu(g_raw)
    return o, st_new


def _hgrn_chunk_kernel(q_ref, f_ref, v_ref, g_ref, loglb_ref, log1m_ref, onem_ref, nw_ref, s0_ref,
                       o_ref, sfin_ref, st_ref, *, n_chunks, hg):
    t = pl.program_id(2)
    c = HGRN_CHUNK

    @pl.when(t == 0)
    def _():
        for hh in range(hg):
            st_ref[hh] = s0_ref[0, hh].T

    loglb, log1m, onem, nw = loglb_ref[0], log1m_ref[0], onem_ref[0], nw_ref[...]
    row = lax.broadcasted_iota(I32, (c, c), 0)
    col = lax.broadcasted_iota(I32, (c, c), 1)
    tri = jnp.where(col <= row, 1.0, 0.0).astype(BF16)
    sub_row = lax.broadcasted_iota(I32, (HGRN_SUB // 2, 1), 0)

    def chunk(ci, carry):
        rows = pl.ds(pl.multiple_of(ci * c, c), c)
        outs = []
        for hh in range(hg):
            ln = slice(hh * LANES, (hh + 1) * LANES)
            o, st_new = _hgrn_head_chunk(q_ref[rows, ln], f_ref[rows, ln], v_ref[rows, ln], g_ref[rows, ln],
                                         loglb[:, ln], log1m[:, ln], onem[:, ln], nw, st_ref[hh], tri, sub_row)
            st_ref[hh] = st_new
            outs.append(o)
        o_ref[rows, :] = jnp.concatenate(outs, axis=1).astype(o_ref.dtype)
        return carry

    lax.fori_loop(0, n_chunks, chunk, 0)

    @pl.when(t == pl.num_programs(2) - 1)
    def _():
        for hh in range(hg):
            sfin_ref[0, hh] = st_ref[hh].T


def hgrn_chunked(proj, loglb, log1m, onem, norm_w, s0, *, batch, seq, tb, hg=8):
    n = proj.shape[0]
    heads = proj.shape[1] // (4 * LANES)
    tpb = seq // tb
    ngrp = heads // hg
    w = hg * LANES
    row_blk = lambda off: pl.BlockSpec((tb, w), lambda b, h, t: (b * tpb + t, off * ngrp + h))
    vec = pl.BlockSpec((1, 1, w), lambda b, h, t: (0, 0, h))
    st_spec = pl.BlockSpec((1, hg, LANES, LANES), lambda b, h, t: (b, h, 0, 0))
    return pl.pallas_call(
        functools.partial(_hgrn_chunk_kernel, n_chunks=tb // HGRN_CHUNK, hg=hg),
        out_shape=(jax.ShapeDtypeStruct((n, heads * LANES), BF16),
                   jax.ShapeDtypeStruct(s0.shape, F32)),
        grid=(batch, ngrp, tpb),
        in_specs=[row_blk(0), row_blk(1), row_blk(2), row_blk(3), vec, vec, vec,
                  pl.BlockSpec((1, LANES), lambda b, h, t: (0, 0)), st_spec],
        out_specs=(pl.BlockSpec((tb, w), lambda b, h, t: (b * tpb + t, h)), st_spec),
        scratch_shapes=[pltpu.VMEM((hg, LANES, LANES), F32)],
        compiler_params=_params("parallel", "parallel", "arbitrary"),
        name="hgrn_chunked",
    )(proj, proj, proj, proj, loglb, log1m, onem, norm_w, s0)


def _hgrn_step_kernel(q_ref, f_ref, v_ref, g_ref, loglb_ref, log1m_ref, onem_ref, nw_ref, s_ref,
                      o_ref, snew_ref, *, bb):
    f = f_ref[...]
    q = _silu(q_ref[...])
    v = v_ref[...]
    x1 = loglb_ref[0]
    x2 = log1m_ref[0] + _log_sigmoid(f)
    mx = jnp.maximum(x1, x2)
    decay = jnp.exp(mx + jnp.log1p(jnp.exp(jnp.minimum(x1, x2) - mx)))
    kk = onem_ref[0] * _sigmoid(-f)
    q_t, d_t, k_t = q.T, decay.T, kk.T
    rows = []
    for e in range(bb):
        s_new = s_ref[e, 0] * d_t[:, e:e + 1] + k_t[:, e:e + 1] * v[e:e + 1, :]
        snew_ref[e, 0] = s_new
        rows.append(jnp.sum(q_t[:, e:e + 1] * s_new, axis=0, keepdims=True))
    o = jnp.concatenate(rows, axis=0)
    o = o * lax.rsqrt(jnp.mean(o * o, axis=-1, keepdims=True) + RMS_EPS) * nw_ref[...] * _silu(g_ref[...])
    o_ref[...] = o.astype(o_ref.dtype)


def hgrn_step(proj, loglb, log1m, onem, norm_w, state, *, bb=16):
    n = proj.shape[0]
    heads = proj.shape[1] // (4 * LANES)
    row_blk = lambda off: pl.BlockSpec((bb, LANES), lambda i, h: (i, off * heads + h))
    vec = pl.BlockSpec((1, 1, LANES), lambda i, h: (0, 0, h))
    st_spec = pl.BlockSpec((bb, 1, LANES, LANES), lambda i, h: (i, h, 0, 0))
    return pl.pallas_call(
        functools.partial(_hgrn_step_kernel, bb=bb),
        out_shape=(jax.ShapeDtypeStruct((n, heads * LANES), BF16),
                   jax.ShapeDtypeStruct(state.shape, F32)),
        grid=(n // bb, heads),
        in_specs=[row_blk(0), row_blk(1), row_blk(2), row_blk(3), vec, vec, vec,
                  pl.BlockSpec((1, LANES), lambda i, h: (0, 0)), st_spec],
        out_specs=(pl.BlockSpec((bb, LANES), lambda i, h: (i, h)), st_spec),
        compiler_params=_params("parallel", "parallel"),
        name="hgrn_step",
    )(proj, proj, proj, proj, loglb, log1m, onem, norm_w, state)


def _proj_ln_kernel(a_ref, w_ref, x_ref, gm_ref, lg_ref, lb_ref, o_ref, *, alpha):
    y = _dot(a_ref[...], w_ref[...])
    o_ref[...] = _layer_norm(alpha * x_ref[...] + gm_ref[0] * y, lg_ref[...], lb_ref[...])


def proj_ln(a, w, x, gm, ln_g, ln_b, *, rows_per_group, tm, alpha):
    n, d = x.shape
    tiles_per_group = rows_per_group // tm
    r = gm.shape[1]
    row = pl.BlockSpec((tm, d), lambda i: (i, 0))
    vec = pl.BlockSpec((1, d), lambda i: (0, 0))
    return pl.pallas_call(
        functools.partial(_proj_ln_kernel, alpha=alpha),
        out_shape=jax.ShapeDtypeStruct((n, d), F32),
        grid=(n // tm,),
        in_specs=[row, pl.BlockSpec((d, d), lambda i: (0, 0)), row,
                  pl.BlockSpec((1, r, d), lambda i: (i // tiles_per_group, 0, 0)), vec, vec],
        out_specs=row,
        compiler_params=_params("parallel"),
        name="proj_ln",
    )(a, w, x, gm, ln_g.reshape(1, d), ln_b.reshape(1, d))


def _lambda_value(lam_ref, lam_init):
    lv = lam_ref[...]
    s1 = jnp.sum(lv[0:1] * lv[1:2], axis=1, keepdims=True)
    s2 = jnp.sum(lv[2:3] * lv[3:4], axis=1, keepdims=True)
    return jnp.exp(s1) - jnp.exp(s2) + lam_init


def _attn_prefill_kernel(lam_ref, q_ref, k_ref, v_ref, w_ref, o_ref, *, tq, lam_init, hpb):
    i = pl.program_id(2)
    lane = lax.broadcasted_iota(I32, (1, LANES), 1)
    keep = lax.broadcasted_iota(I32, (tq, tq), 1) <= lax.broadcasted_iota(I32, (tq, tq), 0)
    q_maps = []
    for hh in range(hpb):
        q = q_ref[:, hh * LANES:(hh + 1) * LANES]
        zero = jnp.zeros_like(q)
        q_maps += [jnp.where(lane < DIFF_HEAD_DIM, q, zero), jnp.where(lane < DIFF_HEAD_DIM, zero, q)]

    def update(s, vj, m, l, acc):
        m_new = jnp.maximum(m, jnp.max(s, axis=1, keepdims=True))
        alpha = jnp.exp(m - m_new)
        p = jnp.exp(s - m_new)
        l = alpha * l + jnp.sum(p, axis=1, keepdims=True)
        acc = alpha * acc + _dot(p.astype(BF16), vj)
        return m_new, l, acc

    def block(j, carry, masked):
        rows = pl.ds(pl.multiple_of(j * tq, tq), tq)
        out = ()
        for hh in range(hpb):
            ln = slice(hh * LANES, (hh + 1) * LANES)
            kj, vj = k_ref[rows, ln], v_ref[rows, ln]
            for c in range(2):
                s = _dot_nt(q_maps[2 * hh + c], kj)
                if masked:
                    s = jnp.where(keep, s, NEG_BIG)
                st = 3 * (2 * hh + c)
                out += update(s, vj, *carry[st:st + 3])
        return out

    m0 = jnp.full((tq, 1), NEG_BIG, F32)
    l0 = jnp.zeros((tq, 1), F32)
    a0 = jnp.zeros((tq, LANES), F32)
    carry = lax.fori_loop(0, i, lambda j, c: block(j, c, False), (m0, l0, a0) * (2 * hpb))
    fin = block(i, carry, True)
    lam = _lambda_value(lam_ref, lam_init)
    outs = []
    for hh in range(hpb):
        _, l1, a1, _, l2, a2 = fin[6 * hh:6 * hh + 6]
        o = a1 / l1 - lam * (a2 / l2)
        outs.append(o * lax.rsqrt(jnp.mean(o * o, axis=-1, keepdims=True) + RMS_EPS) * w_ref[...] * (1.0 - lam_init))
    o_ref[...] = jnp.concatenate(outs, axis=1).astype(o_ref.dtype)


def attn_prefill(q, k, v, lam_vec, subln_w, *, batch, seq, tq, lam_init, hpb=2):
    n, hd = q.shape
    heads = hd // LANES
    nq = seq // tq
    kv_spec = pl.BlockSpec((seq, hpb * LANES), lambda b, h, i: (b, h))
    q_spec = pl.BlockSpec((tq, hpb * LANES), lambda b, h, i: (b * nq + i, h))
    return pl.pallas_call(
        functools.partial(_attn_prefill_kernel, tq=tq, lam_init=lam_init, hpb=hpb),
        out_shape=jax.ShapeDtypeStruct((n, hd), BF16),
        grid=(batch, heads // hpb, nq),
        in_specs=[pl.BlockSpec(lam_vec.shape, lambda b, h, i: (0, 0)), q_spec, kv_spec, kv_spec,
                  pl.BlockSpec((1, LANES), lambda b, h, i: (0, 0))],
        out_specs=q_spec,
        compiler_params=_params("parallel", "parallel", "arbitrary"),
        name="attn_prefill",
    )(lam_vec, q, k, v, subln_w.reshape(1, LANES))


def _attn_decode_kernel(pt_ref, lam_ref, q_ref, kn_ref, vn_ref, *rest, lam_init, pg):
    kc_refs, vc_refs = rest[:pg], rest[pg:2 * pg]
    w_ref, o_ref, m_ref, l_ref, acc_ref = rest[2 * pg:]
    p = pl.program_id(1)
    q = q_ref[0]
    heads = q.shape[0]
    lane = lax.broadcasted_iota(I32, (1, LANES), 1)
    lo = lane < DIFF_HEAD_DIM
    q2 = jnp.concatenate([jnp.where(lo, q, 0.0), jnp.where(lo, 0.0, q)], axis=0)

    @pl.when(p == 0)
    def _():
        kn2 = jnp.concatenate([kn_ref[0], kn_ref[0]], axis=0)
        m_ref[...] = jnp.sum(q2 * kn2, axis=-1, keepdims=True)
        l_ref[...] = jnp.ones_like(l_ref)
        acc_ref[...] = jnp.concatenate([vn_ref[0], vn_ref[0]], axis=0)

    n_rows = kc_refs[0].shape[1] * heads
    own = (lax.broadcasted_iota(I32, (2 * heads, n_rows), 1) % heads
           == lax.broadcasted_iota(I32, (2 * heads, n_rows), 0) % heads)
    q16 = q2.astype(BF16)
    s = [jnp.where(own, _dot_nt(q16, kc_refs[j][0].reshape(n_rows, LANES).astype(BF16)), NEG_BIG)
         for j in range(pg)]
    m_old = m_ref[...]
    m_new = m_old
    for j in range(pg):
        m_new = jnp.maximum(m_new, jnp.max(s[j], axis=1, keepdims=True))
    alpha = jnp.exp(m_old - m_new)
    l_new = alpha * l_ref[...]
    acc = alpha * acc_ref[...]
    for j in range(pg):
        pr = jnp.exp(s[j] - m_new)
        l_new = l_new + jnp.sum(pr, axis=1, keepdims=True)
        acc = acc + _dot(pr.astype(BF16), vc_refs[j][0].reshape(n_rows, LANES).astype(BF16))
    l_ref[...] = l_new
    acc_ref[...] = acc
    m_ref[...] = m_new

    @pl.when(p == pl.num_programs(1) - 1)
    def _():
        lam = _lambda_value(lam_ref, lam_init)
        o = (acc_ref[0:heads, :] / l_ref[0:heads, :]
             - lam * (acc_ref[heads:2 * heads, :] / l_ref[heads:2 * heads, :]))
        o = o * lax.rsqrt(jnp.mean(o * o, axis=-1, keepdims=True) + RMS_EPS) * w_ref[...] * (1.0 - lam_init)
        o_ref[0] = o.astype(o_ref.dtype)


def attn_decode(q, k_new, v_new, cache_k, cache_v, page_table, lam_vec, subln_w, *, lam_init):
    b, heads, _ = q.shape
    n_pages = page_table.shape[1]
    page = cache_k.shape[1]
    pg = math.gcd(n_pages, DECODE_PAGES_PER_STEP)
    tok = pl.BlockSpec((1, heads, LANES), lambda i, p, pt: (i, 0, 0))

    def cache(j):
        return pl.BlockSpec((1, page, heads, LANES), lambda i, p, pt: (pt[i * n_pages + p * pg + j], 0, 0, 0))

    grid_spec = pltpu.PrefetchScalarGridSpec(
        num_scalar_prefetch=1,
        grid=(b, n_pages // pg),
        in_specs=[pl.BlockSpec(lam_vec.shape, lambda i, p, pt: (0, 0)), tok, tok, tok]
        + [cache(j) for j in range(pg)] * 2
        + [pl.BlockSpec((1, LANES), lambda i, p, pt: (0, 0))],
        out_specs=tok,
        scratch_shapes=[pltpu.VMEM((2 * heads, 1), F32), pltpu.VMEM((2 * heads, 1), F32),
                        pltpu.VMEM((2 * heads, LANES), F32)],
    )
    return pl.pallas_call(
        functools.partial(_attn_decode_kernel, lam_init=lam_init, pg=pg),
        out_shape=jax.ShapeDtypeStruct(q.shape, BF16),
        grid_spec=grid_spec,
        compiler_params=_params("parallel", "arbitrary"),
        name="attn_decode",
    )(page_table.reshape(-1), lam_vec, q, k_new, v_new, *([cache_k] * pg), *([cache_v] * pg),
      subln_w.reshape(1, LANES))


def _router_kernel(x_ref, sc_ref, sh_ref, wr_ref, br_ref, idx_ref, gate_ref, pos_ref, cnt_ref,
                   tri_ref, carry_ref, *, tm):
    i = pl.program_id(0)

    @pl.when(i == 0)
    def _():
        r = lax.broadcasted_iota(I32, (tm, tm), 0)
        c = lax.broadcasted_iota(I32, (tm, tm), 1)
        tri_ref[...] = jnp.where(c < r, 1.0, 0.0).astype(BF16)
        carry_ref[...] = jnp.zeros_like(carry_ref)

    h = x_ref[...] * (1.0 + sc_ref[0]) + sh_ref[0]
    h_hi = h.astype(BF16)
    h_lo = (h - h_hi.astype(F32)).astype(BF16)
    w = wr_ref[...]
    w_hi = w.astype(BF16)
    w_lo = (w - w_hi.astype(F32)).astype(BF16)
    logits = _dot(h_hi, w_hi) + _dot(h_lo, w_hi) + _dot(h_hi, w_lo) + br_ref[...]
    lane = lax.broadcasted_iota(I32, (1, LANES), 1)
    work = jnp.where(lane < N_EXPERTS, logits, NEG_BIG)
    tops, idxs = [], []
    member = jnp.zeros((tm, LANES), F32)
    for _ in range(TOP_K):
        mx = jnp.max(work, axis=1, keepdims=True)
        ix = jnp.min(jnp.where(work == mx, lane, LANES), axis=1, keepdims=True)
        hit = lane == ix
        member = jnp.where(hit, 1.0, member)
        work = jnp.where(hit, NEG_BIG, work)
        tops.append(mx)
        idxs.append(ix)
    ex = [jnp.exp(t - tops[0]) for t in tops]
    denom = ex[0] + ex[1] + ex[2] + ex[3]
    rank = _dot(tri_ref[...], member.astype(BF16)) + carry_ref[...]
    idx_out = jnp.zeros((tm, LANES), I32)
    gate_out = jnp.zeros((tm, LANES), F32)
    pos_out = jnp.zeros((tm, LANES), I32)
    for k in range(TOP_K):
        pk = jnp.sum(jnp.where(lane == idxs[k], rank, 0.0), axis=1, keepdims=True)
        idx_out = jnp.where(lane == k, idxs[k], idx_out)
        gate_out = jnp.where(lane == k, ex[k] / denom, gate_out)
        pos_out = jnp.where(lane == k, pk.astype(I32), pos_out)
    idx_ref[...] = idx_out
    gate_ref[...] = gate_out
    pos_ref[...] = pos_out
    carry_ref[...] = carry_ref[...] + jnp.sum(member, axis=0, keepdims=True)
    cnt_ref[...] = carry_ref[...].astype(I32)


def router(x, sc, sh, w_r, b_r, *, rows_per_group, tm):
    n, d = x.shape
    tiles_per_group = rows_per_group // tm
    r = sc.shape[1]
    w_pad = jnp.zeros((d, LANES), F32).at[:, :N_EXPERTS].set(w_r)
    b_pad = jnp.zeros((1, LANES), F32).at[0, :N_EXPERTS].set(b_r)
    mod = pl.BlockSpec((1, r, d), lambda i: (i // tiles_per_group, 0, 0))
    wide = pl.BlockSpec((tm, LANES), lambda i: (i, 0))
    return pl.pallas_call(
        functools.partial(_router_kernel, tm=tm),
        out_shape=(jax.ShapeDtypeStruct((n, LANES), I32), jax.ShapeDtypeStruct((n, LANES), F32),
                   jax.ShapeDtypeStruct((n, LANES), I32), jax.ShapeDtypeStruct((1, LANES), I32)),
        grid=(n // tm,),
        in_specs=[pl.BlockSpec((tm, d), lambda i: (i, 0)), mod, mod,
                  pl.BlockSpec((d, LANES), lambda i: (0, 0)), pl.BlockSpec((1, LANES), lambda i: (0, 0))],
        out_specs=(wide, wide, wide, pl.BlockSpec((1, LANES), lambda i: (0, 0))),
        scratch_shapes=[pltpu.VMEM((tm, tm), BF16), pltpu.VMEM((1, LANES), F32)],
        compiler_params=_params("arbitrary"),
        name="moe_router",
    )(x, sc, sh, w_pad, b_pad)


def _dispatch_kernel(gap_lo_ref, gap_hi_ref, x_ref, sc_ref, sh_ref, dest_ref, xs_ref,
                     h_ref, zero_ref, idx_ref, sem_ref, isem_ref, *, tm, n_gap_rows):
    i = pl.program_id(0)
    n_idx = tm * TOP_K
    idx_copy = pltpu.make_async_copy(dest_ref.at[0, 0], idx_ref, isem_ref)
    idx_copy.start()
    h = x_ref[...] * (1.0 + sc_ref[0]) + sh_ref[0]
    lg = h.shape[1] // LANES
    for g in range(lg):
        h_ref[pl.ds(g, tm, stride=lg), :] = h[:, g * LANES:(g + 1) * LANES]
    idx_copy.wait()

    def row_copy(src, src_tok, dst_tok):
        return pltpu.make_async_copy(_token_rows(src, src_tok, lg), _token_rows(xs_ref, dst_tok, lg), sem_ref)

    def issue(r, carry):
        for k in range(TOP_K):
            row_copy(h_ref, r, idx_ref[r * TOP_K + k]).start(priority=k % DMA_PRIORITIES)
        return carry

    lax.fori_loop(0, tm, issue, 0, unroll=DMA_UNROLL)

    def drain(r, carry):
        row_copy(h_ref, 0, 0).wait()
        return carry

    lax.fori_loop(0, n_idx, drain, 0, unroll=DMA_UNROLL)

    @pl.when(i == 0)
    def _():
        zero_ref[...] = jnp.zeros_like(zero_ref)

        def per_expert(e, carry):
            def fill(rw, c2):
                row_copy(zero_ref, 0, rw).start()
                return c2
            return lax.fori_loop(gap_lo_ref[e], gap_hi_ref[e], fill, carry)

        lax.fori_loop(0, N_EXPERTS, per_expert, 0)

        def drain_zero(r, carry):
            row_copy(zero_ref, 0, 0).wait()
            return carry

        lax.fori_loop(0, n_gap_rows, drain_zero, 0)


def dispatch(x, sc, sh, dest, gap_lo, gap_hi, *, rows_per_group, tm, n_slots_padded):
    n, d = x.shape
    tiles_per_group = rows_per_group // tm
    r = sc.shape[1]
    n_idx = tm * TOP_K
    mod = pl.BlockSpec((1, r, d), lambda i, lo, hi: (i // tiles_per_group, 0, 0))
    grid_spec = pltpu.PrefetchScalarGridSpec(
        num_scalar_prefetch=2,
        grid=(n // tm,),
        in_specs=[pl.BlockSpec((tm, d), lambda i, lo, hi: (i, 0)), mod, mod,
                  pl.BlockSpec((1, 1, n_idx), lambda i, lo, hi: (i, 0, 0))],
        out_specs=pl.BlockSpec(memory_space=pl.ANY),
        scratch_shapes=[pltpu.VMEM((tm * (d // LANES), LANES), F32), pltpu.VMEM((d // LANES, LANES), F32),
                        pltpu.SMEM((n_idx,), I32), pltpu.SemaphoreType.DMA, pltpu.SemaphoreType.DMA],
    )
    return pl.pallas_call(
        functools.partial(_dispatch_kernel, tm=tm, n_gap_rows=n_slots_padded - n * TOP_K),
        out_shape=jax.ShapeDtypeStruct((n_slots_padded * (d // LANES), LANES), F32),
        grid_spec=grid_spec,
        compiler_params=_params("arbitrary"),
        name="moe_dispatch",
    )(gap_lo, gap_hi, x, sc, sh, dest.reshape(n // tm, 1, n_idx))


def _expert_ffn_kernel(be_ref, nu_ref, xs_ref, wgu_ref, bgu_ref, wd_ref, bd_ref, ys_ref, wgu16_ref, wd16_ref,
                       *, blk):
    i = pl.program_id(0)
    active = i < nu_ref[0]

    @pl.when(jnp.logical_not(active))
    def _():
        ys_ref[...] = jnp.zeros_like(ys_ref)

    @pl.when(active & ((i == 0) | (be_ref[i] != be_ref[jnp.maximum(i - 1, 0)])))
    def _():
        half = GU_GROUP // 2
        r = lax.broadcasted_iota(I32, (GU_GROUP, GU_GROUP), 0)
        c = lax.broadcasted_iota(I32, (GU_GROUP, GU_GROUP), 1)
        perm = jnp.where(r == jnp.where(c < half, 2 * c, 2 * (c - half) + 1), 1.0, 0.0).astype(BF16)
        for g in range(wgu_ref.shape[2] // GU_GROUP):
            cols = slice(g * GU_GROUP, (g + 1) * GU_GROUP)
            wgu16_ref[:, cols] = _dot(wgu_ref[0, :, cols].astype(BF16), perm).astype(BF16)
        wd16_ref[...] = wd_ref[0].astype(BF16)

    @pl.when(active)
    def _():
        lg = xs_ref.shape[0] // blk
        x = jnp.concatenate([xs_ref[pl.ds(g, blk, stride=lg), :] for g in range(lg)], axis=1).astype(BF16)
        gu = _dot(x, wgu16_ref[...]) + bgu_ref[0]
        half = GU_GROUP // 2
        acts = []
        for g in range(gu.shape[1] // GU_GROUP):
            gate = jnp.minimum(gu[:, g * GU_GROUP:g * GU_GROUP + half], SWIGLU_LIMIT)
            up = jnp.clip(gu[:, g * GU_GROUP + half:(g + 1) * GU_GROUP], -SWIGLU_LIMIT, SWIGLU_LIMIT)
            acts.append((up + 1.0) * (gate * _sigmoid(gate * SWIGLU_ALPHA)))
        act = jnp.concatenate(acts, axis=1).astype(BF16)
        y = _dot(act, wd16_ref[...]) + bd_ref[0]
        for g in range(lg):
            ys_ref[pl.ds(g, blk, stride=lg), :] = y[:, g * LANES:(g + 1) * LANES]


def expert_ffn(xs, block_expert, n_used_blocks, wgu, bgu, wd, bd, *, blk, expert_base):
    d, de2 = wgu.shape[1], wgu.shape[2]
    lg = d // LANES
    ns = xs.shape[0] // lg
    rows_in = pl.BlockSpec((blk * lg, LANES), lambda i, be, nu: (jnp.minimum(i, nu[0] - 1), 0))
    rows_out = pl.BlockSpec((blk * lg, LANES), lambda i, be, nu: (i, 0))
    wspec = lambda a, b: pl.BlockSpec((1, a, b), lambda i, be, nu: (expert_base + be[i], 0, 0))
    grid_spec = pltpu.PrefetchScalarGridSpec(
        num_scalar_prefetch=2,
        grid=(ns // blk,),
        in_specs=[rows_in, wspec(d, de2), wspec(1, de2), wspec(de2 // 2, d), wspec(1, d)],
        out_specs=rows_out,
        scratch_shapes=[pltpu.VMEM((d, de2), BF16), pltpu.VMEM((de2 // 2, d), BF16)],
    )
    return pl.pallas_call(
        functools.partial(_expert_ffn_kernel, blk=blk),
        out_shape=jax.ShapeDtypeStruct(xs.shape, F32),
        grid_spec=grid_spec,
        compiler_params=_params("arbitrary"),
        name="moe_expert_ffn",
    )(block_expert, n_used_blocks, xs, wgu, bgu, wd, bd)


def _combine_ln_kernel(x_ref, gm_ref, gate_ref, dest_ref, ys_ref, lg_ref, lb_ref, o_ref,
                       buf_ref, idx_ref, sem_ref, isem_ref, *, tm, alpha):
    n_idx = tm * TOP_K
    idx_copy = pltpu.make_async_copy(dest_ref.at[0, 0], idx_ref, isem_ref)
    idx_copy.start()
    idx_copy.wait()

    lg = buf_ref.shape[1] // tm

    def row_copy(src_tok, k, r):
        return pltpu.make_async_copy(_token_rows(ys_ref, src_tok, lg), _token_rows(buf_ref.at[k], r, lg), sem_ref)

    def issue(r, carry):
        for k in range(TOP_K):
            row_copy(idx_ref[r * TOP_K + k], k, r).start(priority=k % DMA_PRIORITIES)
        return carry

    lax.fori_loop(0, tm, issue, 0, unroll=DMA_UNROLL)

    def drain(r, carry):
        row_copy(0, 0, 0).wait()
        return carry

    lax.fori_loop(0, n_idx, drain, 0, unroll=DMA_UNROLL)
    gates = gate_ref[...]
    cols = []
    for g in range(lg):
        yg = gates[:, 0:1] * buf_ref[0, pl.ds(g, tm, stride=lg), :]
        for k in range(1, TOP_K):
            yg = yg + gates[:, k:k + 1] * buf_ref[k, pl.ds(g, tm, stride=lg), :]
        cols.append(yg)
    y = jnp.concatenate(cols, axis=1)
    o_ref[...] = _layer_norm(alpha * x_ref[...] + gm_ref[0] * y, lg_ref[...], lb_ref[...])


def combine_ln(x, gm, gates, dest, ys, ln_g, ln_b, *, rows_per_group, tm, alpha):
    n, d = x.shape
    tiles_per_group = rows_per_group // tm
    r = gm.shape[1]
    n_idx = tm * TOP_K
    row = pl.BlockSpec((tm, d), lambda i: (i, 0))
    vec = pl.BlockSpec((1, d), lambda i: (0, 0))
    return pl.pallas_call(
        functools.partial(_combine_ln_kernel, tm=tm, alpha=alpha),
        out_shape=jax.ShapeDtypeStruct((n, d), F32),
        grid=(n // tm,),
        in_specs=[row, pl.BlockSpec((1, r, d), lambda i: (i // tiles_per_group, 0, 0)),
                  pl.BlockSpec((tm, LANES), lambda i: (i, 0)),
                  pl.BlockSpec((1, 1, n_idx), lambda i: (i, 0, 0)),
                  pl.BlockSpec(memory_space=pl.ANY), vec, vec],
        out_specs=row,
        scratch_shapes=[pltpu.VMEM((TOP_K, tm * (d // LANES), LANES), F32), pltpu.SMEM((n_idx,), I32),
                        pltpu.SemaphoreType.DMA, pltpu.SemaphoreType.DMA],
        compiler_params=_params("arbitrary"),
        name="moe_combine_ln",
    )(x, gm, gates, dest.reshape(n // tm, 1, n_idx), ys, ln_g.reshape(1, d), ln_b.reshape(1, d))


def _moe_layer(x, sc, sh, gm, ln_g, ln_b, w_r, b_r, wgu, bgu, wd, bd, *, expert_base, rows_per_group, tm, blk,
               alpha):
    n, d = x.shape
    idx_w, gate_w, pos_w, cnt_w = router(x, sc, sh, w_r, b_r, rows_per_group=rows_per_group, tm=tm)
    counts = cnt_w[0, :N_EXPERTS]
    top_idx, pos = idx_w[:, :TOP_K], pos_w[:, :TOP_K]
    n_slots = n * TOP_K
    n_blocks = -(-(n_slots + N_EXPERTS * (blk - 1)) // blk)
    padded = (counts + blk - 1) // blk * blk
    pad_end = jnp.cumsum(padded)
    pad_start = pad_end - padded
    onehot = top_idx[..., None] == jnp.arange(N_EXPERTS, dtype=I32)
    dest = (jnp.sum(jnp.where(onehot, pad_start, 0), axis=-1) + pos).astype(I32)
    block_start = jnp.arange(n_blocks, dtype=I32) * blk
    block_expert = jnp.minimum(
        jnp.sum((pad_end[None, :] <= block_start[:, None]).astype(I32), axis=1), N_EXPERTS - 1).astype(I32)
    gap_lo = (pad_start + counts).astype(I32)
    gap_hi = jnp.concatenate([pad_start[1:], jnp.full((1,), n_blocks * blk, I32)]).astype(I32)
    xs = dispatch(x, sc, sh, dest, gap_lo, gap_hi, rows_per_group=rows_per_group, tm=tm,
                  n_slots_padded=n_blocks * blk)
    n_used_blocks = (pad_end[-1:] // blk).astype(I32)
    ys = expert_ffn(xs, block_expert, n_used_blocks, wgu, bgu, wd, bd, blk=blk, expert_base=expert_base)
    return combine_ln(x, gm, gate_w, dest, ys, ln_g, ln_b, rows_per_group=rows_per_group, tm=tm, alpha=alpha)


def _rope_tables(pos):
    inv_freq = ROPE_THETA ** (-jnp.arange(ROPE_HALF, dtype=F32) / ROPE_HALF)
    ang = pos.astype(F32)[:, None] * inv_freq[None, :]
    cos, sin = jnp.cos(ang), jnp.sin(ang)
    reps = LANES // DIFF_HEAD_DIM
    return (jnp.concatenate([cos, cos] * reps, axis=1), jnp.concatenate([-sin, sin] * reps, axis=1))


def _trunk(x, mods, kv_mods, past_len, s0, cache, wts, *, batch, seq, tm, blk):
    n, d = x.shape
    depth = wts["ada_w"].shape[0]
    n_a = wts["hgrn_w_in"].shape[0]
    alpha = (2 * depth) ** 0.25
    rows_per_group = seq if seq >= tm else n
    kw = dict(rows_per_group=rows_per_group, tm=tm)
    decode = seq == 1
    pos = past_len + (jnp.zeros((n,), I32) if decode else jnp.arange(seq, dtype=I32))
    rope = _rope_tables(pos)
    heads = d // LANES
    states = []
    k_new = v_new = k16 = v16 = None
    for l in range(depth):
        sh_m, sc_m, g_m, sh_f, sc_f, g_f = mods[l]
        if l < n_a:
            proj, = modmm(x, sc_m, sh_m, wts["hgrn_w_in"][l], out_dtypes=(F32,), **kw)
            lbs = [a[l][None] for a in wts["lb"]]
            nw = wts["hgrn_norm_w"][l].reshape(1, LANES)
            if decode:
                o, s_l = hgrn_step(proj, *lbs, nw, s0[l])
            else:
                o, s_l = hgrn_chunked(proj, *lbs, nw, s0[l], batch=batch, seq=seq, tb=min(seq, 512))
            states.append(s_l)
            w_out = wts["hgrn_w_out"][l]
        else:
            j = l - n_a
            lam_init = 0.8 - 0.6 * math.exp(-0.3 * l)
            if l == n_a:
                kv_shift, kv_scale = kv_mods
                half = wts["kv_w"].shape[1] // 2
                k_new, k16 = modmm(x, kv_scale, kv_shift, wts["kv_w"][:, :half], out_dtypes=(F32, BF16),
                                   rope=rope, **kw)
                v_new, v16 = modmm(x, kv_scale, kv_shift, wts["kv_w"][:, half:], out_dtypes=(F32, BF16), **kw)
            q_scale = DIFF_HEAD_DIM ** -0.5
            if decode:
                q, = modmm(x, sc_m, sh_m, wts["diff_w_q"][j], out_dtypes=(F32,), rope=rope, out_scale=q_scale, **kw)
                o = attn_decode(q.reshape(n, heads, LANES), k_new.reshape(n, heads, LANES),
                                v_new.reshape(n, heads, LANES), cache[0], cache[1], cache[2],
                                wts["diff_lambda"][j], wts["diff_subln_w"][j], lam_init=lam_init)
                o = o.reshape(n, d)
            else:
                q, = modmm(x, sc_m, sh_m, wts["diff_w_q"][j], out_dtypes=(BF16,), rope=rope, out_scale=q_scale, **kw)
                o = attn_prefill(q, k16, v16, wts["diff_lambda"][j], wts["diff_subln_w"][j],
                                 batch=batch, seq=seq, tq=min(seq, 1024), lam_init=lam_init, hpb=1)
            w_out = wts["diff_w_out"][j]
        x = proj_ln(o, w_out, x, g_m, wts["ln_g"][l, 0], wts["ln_b"][l, 0], alpha=alpha, **kw)
        x = _moe_layer(x, sc_f, sh_f, g_f, wts["ln_g"][l, 1], wts["ln_b"][l, 1],
                       wts["moe_w_router"][l], wts["moe_b_router"][l],
                       wts["moe_w_gu"], wts["moe_b_gu"], wts["moe_w_down"], wts["moe_b_down"],
                       expert_base=l * N_EXPERTS, alpha=alpha, blk=blk, **kw)
    return x, k_new, v_new, jnp.stack(states)


def kernel(x_prompt, x_sample, cache_k, cache_v, state_hgrn, page_table, c_prompt, c_sample, ada_w, ada_b, ln_g, ln_b, hgrn_w_in, hgrn_lb_logits, hgrn_norm_w, hgrn_w_out, kv_ada_w, kv_ada_b, kv_w, diff_w_q, diff_lambda, diff_subln_w, diff_w_out, moe_w_router, moe_b_router, moe_w_gu, moe_b_gu, moe_w_down, moe_b_down):
    b_p, t_p, d = x_prompt.shape
    b_d, t_d, _ = x_sample.shape
    depth = ada_w.shape[0]
    n_a = hgrn_w_in.shape[0]
    heads = d // LANES
    de = moe_w_down.shape[2]

    m_rows = -(-(b_d + b_p) // 8) * 8
    c_all = jnp.zeros((m_rows, d), F32).at[:b_d].set(c_sample).at[b_d:b_d + b_p].set(c_prompt)
    mod_all = adaln(c_all, ada_w, ada_b)
    kv_mod_all = adaln(c_all, kv_ada_w[None], kv_ada_b[None])[0]

    def split_mods(m, n_vec, lo, hi, per_row):
        parts = [m[lo:hi, v * d:(v + 1) * d] for v in range(n_vec)]
        return [p[None] if per_row else p[:, None, :] for p in parts]

    mods_p = [split_mods(mod_all[l], 6, b_d, b_d + b_p, False) for l in range(depth)]
    mods_s = [split_mods(mod_all[l], 6, 0, b_d, True) for l in range(depth)]
    kv_mods_p = split_mods(kv_mod_all, 2, b_d, b_d + b_p, False)
    kv_mods_s = split_mods(kv_mod_all, 2, 0, b_d, True)

    loglb, log1m, onem = hgrn_lower_bounds(hgrn_lb_logits)

    n_exp = moe_w_gu.shape[1]
    half = GU_GROUP // 2
    b_gu = moe_b_gu.reshape(depth, n_exp, 2 * de // GU_GROUP, half, 2)
    b_gu = jnp.swapaxes(b_gu, -1, -2).reshape(depth * n_exp, 1, 2 * de)

    wts = dict(
        ada_w=ada_w, ln_g=ln_g, ln_b=ln_b,
        hgrn_w_in=hgrn_w_in.astype(BF16), lb=(loglb, log1m, onem), hgrn_norm_w=hgrn_norm_w,
        hgrn_w_out=hgrn_w_out.astype(BF16), kv_w=kv_w.astype(BF16), diff_w_q=diff_w_q.astype(BF16),
        diff_lambda=diff_lambda, diff_subln_w=diff_subln_w, diff_w_out=diff_w_out.astype(BF16),
        moe_w_router=moe_w_router, moe_b_router=moe_b_router,
        moe_w_gu=moe_w_gu.reshape(depth * n_exp, d, 2 * de), moe_b_gu=b_gu,
        moe_w_down=moe_w_down.reshape(depth * n_exp, de, d), moe_b_down=moe_b_down.reshape(depth * n_exp, 1, d),
    )

    s0_prompt = jnp.zeros((n_a, b_p, heads, LANES, LANES), F32)
    y_p, k_p, v_p, st_p = _trunk(x_prompt.reshape(b_p * t_p, d), mods_p, kv_mods_p, 0, s0_prompt, None, wts,
                                 batch=b_p, seq=t_p, tm=min(512, t_p), blk=256)
    past_len = page_table.shape[1] * cache_k.shape[1]
    y_s, k_s, v_s, st_s = _trunk(x_sample.reshape(b_d * t_d, d), mods_s, kv_mods_s, past_len, state_hgrn,
                                 (cache_k, cache_v, page_table), wts, batch=b_d, seq=t_d, tm=b_d, blk=16)
    dv = d // heads
    return (y_p.reshape(b_p, t_p, d), y_s.reshape(b_d, t_d, d),
            k_p.reshape(b_p, t_p, heads, dv), v_p.reshape(b_p, t_p, heads, dv), st_p,
            k_s.reshape(b_d, t_d, heads, dv), v_s.reshape(b_d, t_d, heads, dv), st_s)
```
